```python
import jax, jax.numpy as jnp
from jax import lax
import numpy as np

D_MODEL = 1024
BATCH = 8
SEQ = 4096
DEPTH = 2

HEAD_DIM = 64
D_MIX = D_MODEL
A_Q_HEADS = D_MIX // (2 * HEAD_DIM)
A_KV_HEADS = max(1, A_Q_HEADS // 4)
B_HEADS = D_MIX // (2 * HEAD_DIM)
A_WIDTH = A_Q_HEADS * HEAD_DIM
A_KV_WIDTH = A_KV_HEADS * HEAD_DIM
B_WIDTH = B_HEADS * HEAD_DIM
A_HALF_WINDOW = 128
B_BRANCHES = ((128, 1), (512, 4), (2048, 16))
ROPE_THETA = 500000.0
ROPE_DIM = HEAD_DIM // 4
D_FF = int(round(8 * D_MODEL / 3 / 256)) * 256
N_MOD = 9
IN_SIZES = (A_WIDTH, A_KV_WIDTH, A_KV_WIDTH, B_WIDTH, B_WIDTH, B_WIDTH)
IN_SPLITS = tuple(int(v) for v in np.cumsum(IN_SIZES)[:-1])
D_IN = int(sum(IN_SIZES))
EPS = 1e-6
NEG_INF = -1e30

kernel_name = "hybrid_parallel_swa_dilated_macaron_adaln"


def rmsnorm(x, g):
    xf = x.astype(jnp.float32)
    y = xf * lax.rsqrt(jnp.mean(xf * xf, axis=-1, keepdims=True) + EPS)
    return (y * g.astype(jnp.float32)).astype(x.dtype)


def modulate(h, shift, scale):
    return h * (1 + scale) + shift


def swiglu(h, wi, wo):
    gate, up = jnp.split(h @ wi, 2, axis=-1)
    return (jax.nn.silu(gate) * up) @ wo


def apply_rope(t, cos, sin):
    half = ROPE_DIM // 2
    tr = t[..., :ROPE_DIM].astype(jnp.float32)
    t1, t2 = tr[..., :half], tr[..., half:]
    rot = jnp.concatenate([t1 * cos - t2 * sin, t2 * cos + t1 * sin], axis=-1)
    return jnp.concatenate([rot.astype(t.dtype), t[..., ROPE_DIM:]], axis=-1)


def banded_attention(q, k, v, half_window, sink=None):
    n, L, hq, dh = q.shape
    hkv = k.shape[2]
    g = hq // hkv
    w = half_window
    nb = -(-L // w)
    pad = nb * w - L
    qb = jnp.pad(q * (dh ** -0.5), ((0, 0), (0, pad), (0, 0), (0, 0))).reshape(n, nb, w, hkv, g, dh)
    kp = jnp.pad(k, ((0, 0), (w, w + pad), (0, 0), (0, 0)))
    vp = jnp.pad(v, ((0, 0), (w, w + pad), (0, 0), (0, 0)))
    kidx = (jnp.arange(nb) * w)[:, None] + jnp.arange(3 * w)[None, :]
    kb = kp[:, kidx]
    vb = vp[:, kidx]
    s = jnp.einsum("nbqhgd,nbkhd->nbhgqk", qb, kb, preferred_element_type=jnp.float32)
    qpos = (jnp.arange(nb) * w)[:, None] + jnp.arange(w)[None, :]
    kpos = kidx - w
    valid = ((jnp.abs(qpos[:, :, None] - kpos[:, None, :]) <= w)
             & (kpos[:, None, :] >= 0) & (kpos[:, None, :] < L))
    s = jnp.where(valid[None, :, None, None], s, NEG_INF)
    lse = jax.nn.logsumexp(s, axis=-1)
    if sink is None:
        denom = lse
    else:
        denom = jnp.logaddexp(lse, sink.astype(jnp.float32).reshape(hkv, g)[None, None, :, :, None])
    p = jnp.exp(s - denom[..., None]).astype(v.dtype)
    o = jnp.einsum("nbhgqk,nbkhd->nbqhgd", p, vb).reshape(n, nb * w, hq, dh)[:, :L]
    lse = lse.transpose(0, 1, 4, 2, 3).reshape(n, nb * w, hq)[:, :L]
    return o, lse


def dilated_branch(q, k, v, window, dilation):
    b, s, h, dh = q.shape
    L = s // dilation

    def to_sub(t):
        return t.reshape(b, L, dilation, h, dh).transpose(0, 2, 1, 3, 4).reshape(b * dilation, L, h, dh)

    o, lse = banded_attention(to_sub(q), to_sub(k), to_sub(v), window // (2 * dilation))
    o = o.reshape(b, dilation, L, h, dh).transpose(0, 2, 1, 3, 4).reshape(b, s, h, dh)
    lse = lse.reshape(b, dilation, L, h).transpose(0, 2, 1, 3).reshape(b, s, h)
    return o, lse


def token_mixing(h, cos, sin, w_in, sink, onorm_a, onorm_b, w_out):
    b, s, _ = h.shape
    proj = h @ w_in
    qa, ka, va, qb, kb, vb = jnp.split(proj, IN_SPLITS, axis=-1)
    qa = apply_rope(qa.reshape(b, s, A_Q_HEADS, HEAD_DIM), cos, sin)
    ka = apply_rope(ka.reshape(b, s, A_KV_HEADS, HEAD_DIM), cos, sin)
    va = va.reshape(b, s, A_KV_HEADS, HEAD_DIM)
    qb = apply_rope(qb.reshape(b, s, B_HEADS, HEAD_DIM), cos, sin)
    kb = apply_rope(kb.reshape(b, s, B_HEADS, HEAD_DIM), cos, sin)
    vb = vb.reshape(b, s, B_HEADS, HEAD_DIM)
    oa, _ = banded_attention(qa, ka, va, A_HALF_WINDOW, sink)
    outs, lses = [], []
    for window, dilation in B_BRANCHES:
        o, l = dilated_branch(qb, kb, vb, window, dilation)
        outs.append(o)
        lses.append(l)
    wts = jax.nn.softmax(jnp.stack(lses, axis=0), axis=0)
    ob = jnp.sum(wts[..., None] * jnp.stack(outs, axis=0).astype(jnp.float32), axis=0).astype(h.dtype)
    ya = rmsnorm(oa.reshape(b, s, A_WIDTH), onorm_a)
    yb = rmsnorm(ob.reshape(b, s, B_WIDTH), onorm_b)
    return jnp.concatenate([ya, yb], axis=-1) @ w_out


def setup_inputs(seed: int = 0) -> dict:
    key = jax.random.key(seed)
    ks = jax.random.split(key, 20)
    f32 = jnp.float32

    def nrm(k, shape, scale):
        return jax.random.normal(k, shape, f32) * scale

    def gain(k, shape):
        return 1.0 + 0.02 * jax.random.normal(k, shape, f32)

    x = nrm(ks[0], (BATCH, SEQ, D_MODEL), 1.0)
    c = nrm(ks[1], (BATCH, D_MODEL), 1.0)
    positions = (jnp.arange(SEQ, dtype=jnp.int32)[None, :]
                 + jax.random.randint(ks[2], (BATCH, 1), 0, 1024, dtype=jnp.int32))
    return {
        "x": x,
        "c": c,
        "positions": positions,
        "ada_w": nrm(ks[3], (DEPTH, D_MODEL, N_MOD * D_MODEL), 0.5 * D_MODEL ** -0.5),
        "ada_b": nrm(ks[4], (DEPTH, N_MOD * D_MODEL), 0.02),
        "norm_ffn1": gain(ks[5], (DEPTH, D_MODEL)),
        "ffn1_wi": nrm(ks[6], (DEPTH, D_MODEL, 2 * D_FF), D_MODEL ** -0.5),
        "ffn1_wo": nrm(ks[7], (DEPTH, D_FF, D_MODEL), D_FF ** -0.5),
        "norm_mix": gain(ks[8], (DEPTH, D_MODEL)),
        "w_in": nrm(ks[9], (DEPTH, D_MODEL, D_IN), D_MODEL ** -0.5),
        "sink": nrm(ks[10], (DEPTH, A_Q_HEADS), 0.5),
        "onorm_a": gain(ks[11], (DEPTH, A_WIDTH)),
        "onorm_b": gain(ks[12], (DEPTH, B_WIDTH)),
        "w_out": nrm(ks[13], (DEPTH, D_MIX, D_MODEL), D_MIX ** -0.5),
        "norm_ffn2": gain(ks[14], (DEPTH, D_MODEL)),
        "ffn2_wi": nrm(ks[15], (DEPTH, D_MODEL, 2 * D_FF), D_MODEL ** -0.5),
        "ffn2_wo": nrm(ks[16], (DEPTH, D_FF, D_MODEL), D_FF ** -0.5),
        "final_norm": gain(ks[17], (D_MODEL,)),
    }


def reference(x, c, positions, ada_w, ada_b, norm_ffn1, ffn1_wi, ffn1_wo, norm_mix, w_in, sink,
              onorm_a, onorm_b, w_out, norm_ffn2, ffn2_wi, ffn2_wo, final_norm):
    b = x.shape[0]
    inv_freq = ROPE_THETA ** (-jnp.arange(0, ROPE_DIM, 2, dtype=jnp.float32) / ROPE_DIM)
    ang = positions.astype(jnp.float32)[:, :, None, None] * inv_freq
    cos, sin = jnp.cos(ang), jnp.sin(ang)
    c_act = jax.nn.silu(c)
    for l in range(DEPTH):
        mod = (c_act @ ada_w[l] + ada_b[l]).reshape(b, N_MOD, 1, D_MODEL)
        sh1, sc1, g1 = mod[:, 0], mod[:, 1], mod[:, 2]
        sh2, sc2, g2 = mod[:, 3], mod[:, 4], mod[:, 5]
        sh3, sc3, g3 = mod[:, 6], mod[:, 7], mod[:, 8]
        h = modulate(rmsnorm(x, norm_ffn1[l]), sh1, sc1)
        x = x + 0.5 * g1 * swiglu(h, ffn1_wi[l], ffn1_wo[l])
        h = modulate(rmsnorm(x, norm_mix[l]), sh2, sc2)
        x = x + g2 * token_mixing(h, cos, sin, w_in[l], sink[l], onorm_a[l], onorm_b[l], w_out[l])
        h = modulate(rmsnorm(x, norm_ffn2[l]), sh3, sc3)
        x = x + 0.5 * g3 * swiglu(h, ffn2_wi[l], ffn2_wo[l])
    return rmsnorm(x, final_norm)
```

```python
import functools

import numpy as np
import jax
import jax.numpy as jnp
from jax import lax
from jax.experimental import pallas as pl
from jax.experimental.pallas import tpu as pltpu

D_MODEL = 1024
HEAD_DIM = 64
N_HEADS = 8
MIX_WIDTH = N_HEADS * HEAD_DIM
KV_A_WIDTH = 2 * HEAD_DIM
A_HALF_WINDOW = 128
B_BRANCHES = ((128, 1), (512, 4), (2048, 16))
ROPE_THETA = 500000.0
ROPE_DIM = 16
D_FF = 2816
N_MOD = 9
D_IN = 2304
EPS = 1e-6
NEG_INF = -1e30

LANES = 128
Q_ROWS = 128
F_CHUNK = 256
VMEM_LIMIT = 56 * 1024 * 1024

F32 = jnp.float32
BF16 = jnp.bfloat16


def _params(n_axes):
    return pltpu.CompilerParams(
        dimension_semantics=("arbitrary",) * n_axes, vmem_limit_bytes=VMEM_LIMIT)


def _resident(block_shape, index_map):
    return pl.BlockSpec(block_shape, index_map, pipeline_mode=pl.Buffered(1))


def _norm_mod(x, gain, shift, scale):
    y = x * lax.rsqrt(jnp.mean(x * x, axis=-1, keepdims=True) + EPS)
    return (y * gain) * (1.0 + scale) + shift


def _mod_kernel(c_ref, w_ref, b_ref, o_ref):
    ca = jax.nn.silu(c_ref[...]).astype(BF16)
    o_ref[0] = jnp.dot(ca, w_ref[0].astype(BF16), preferred_element_type=F32) + b_ref[0]


def _modulation(c, ada_w, ada_b):
    depth, d, n = ada_w.shape
    b = c.shape[0]
    tn = n // 8
    out = pl.pallas_call(
        _mod_kernel,
        grid=(depth, n // tn),
        in_specs=[
            pl.BlockSpec((b, d), lambda l, j: (0, 0)),
            pl.BlockSpec((1, d, tn), lambda l, j: (l, 0, j)),
            pl.BlockSpec((1, 1, tn), lambda l, j: (l, 0, j)),
        ],
        out_specs=pl.BlockSpec((1, b, tn), lambda l, j: (l, 0, j)),
        out_shape=jax.ShapeDtypeStruct((depth, b, n), F32),
        compiler_params=_params(2),
        name="adaln_mod",
    )(c, ada_w, ada_b.reshape(depth, 1, n))
    return out.reshape(depth, b, N_MOD, d)


def _ffn_kernel(x_ref, mod_ref, g_ref, wi_ref, wo_ref, fg_ref, o_ref, act_ref, *,
                mod_row, final_norm):
    x = x_ref[0]
    shift = mod_ref[0, 0, mod_row:mod_row + 1, :]
    scale = mod_ref[0, 0, mod_row + 1:mod_row + 2, :]
    gate_res = mod_ref[0, 0, mod_row + 2:mod_row + 3, :]
    hb = _norm_mod(x, g_ref[...], shift, scale).astype(BF16)
    for j in range(D_FF // F_CHUNK):
        lo = j * F_CHUNK
        gate = jnp.dot(hb, wi_ref[:, lo:lo + F_CHUNK], preferred_element_type=F32)
        up = jnp.dot(hb, wi_ref[:, D_FF + lo:D_FF + lo + F_CHUNK], preferred_element_type=F32)
        act_ref[:, lo:lo + F_CHUNK] = (jax.nn.silu(gate) * up).astype(BF16)
    y = jnp.dot(act_ref[...], wo_ref[...], preferred_element_type=F32)
    out = x + (0.5 * gate_res) * y
    if final_norm:
        out = out * lax.rsqrt(jnp.mean(out * out, axis=-1, keepdims=True) + EPS) * fg_ref[...]
    o_ref[0] = out


def _ffn(x, mod, layer, mod_row, gain, wi, wo, final_gain, final_norm, tm=512):
    b, s, d = x.shape
    kern = functools.partial(_ffn_kernel, mod_row=mod_row, final_norm=final_norm)
    return pl.pallas_call(
        kern,
        grid=(b, s // tm),
        in_specs=[
            pl.BlockSpec((1, tm, d), lambda i, j: (i, j, 0)),
            pl.BlockSpec((1, 1, N_MOD, d), lambda i, j: (layer, i, 0, 0)),
            pl.BlockSpec((None, 1, d), lambda i, j: (layer, 0, 0)),
            _resident((None, d, 2 * D_FF), lambda i, j: (layer, 0, 0)),
            _resident((None, D_FF, d), lambda i, j: (layer, 0, 0)),
            pl.BlockSpec((1, d), lambda i, j: (0, 0)),
        ],
        out_specs=pl.BlockSpec((1, tm, d), lambda i, j: (i, j, 0)),
        out_shape=jax.ShapeDtypeStruct(x.shape, F32),
        scratch_shapes=[pltpu.VMEM((tm, D_FF), BF16)],
        compiler_params=_params(2),
        name="ffn",
    )(x, mod, gain, wi, wo, final_gain)


def _rope_cs_kernel(pos_ref, inv_ref, cos_ref, sin_ref):
    ang = pos_ref[0].astype(F32) * inv_ref[...]
    cos_ref[0] = jnp.cos(ang)
    sin_ref[0] = jnp.sin(ang)


def _rope_tables(positions):
    b, s = positions.shape
    nf = ROPE_DIM // 2
    rows = s * nf // LANES
    inv = (ROPE_THETA ** (-np.arange(0, ROPE_DIM, 2, dtype=np.float64) / ROPE_DIM)).astype(np.float32)
    inv_tile = jnp.asarray(np.tile(inv, LANES // nf).reshape(1, LANES))
    pos_rep = jnp.repeat(positions, nf, axis=1).reshape(b, rows, LANES)
    spec = pl.BlockSpec((1, rows, LANES), lambda i: (i, 0, 0))
    cos, sin = pl.pallas_call(
        _rope_cs_kernel,
        grid=(b,),
        in_specs=[spec, pl.BlockSpec((1, LANES), lambda i: (0, 0))],
        out_specs=[spec, spec],
        out_shape=[jax.ShapeDtypeStruct((b, rows, LANES), F32)] * 2,
        compiler_params=_params(1),
        name="rope_cos_sin",
    )(pos_rep, inv_tile)
    return cos.reshape(b, s, nf), sin.reshape(b, s, nf)


def _proj_kernel(x_ref, mod_ref, g_ref, cos_ref, sin_ref, w_ref,
                 qa_ref, ka_ref, va_ref, qb_ref, kb_ref, vb_ref):
    x = x_ref[0]
    tm = x.shape[0]
    hb = _norm_mod(x, g_ref[...], mod_ref[0, 0, 3:4, :], mod_ref[0, 0, 4:5, :]).astype(BF16)

    lane = lax.broadcasted_iota(jnp.int32, (tm, LANES), 1)
    in_head = lane & (HEAD_DIM - 1)
    half = ROPE_DIM // 2
    cos8 = cos_ref[0]
    sin8 = sin_ref[0]
    cos_t = jnp.ones((tm, LANES), F32)
    sin_t = jnp.zeros((tm, LANES), F32)
    for f in range(half):
        sel = (in_head < ROPE_DIM) & ((in_head & (half - 1)) == f)
        cos_t = jnp.where(sel, cos8[:, f:f + 1], cos_t)
        sin_t = jnp.where(sel, sin8[:, f:f + 1], sin_t)
    first_half = in_head < half
    sin_t = jnp.where(first_half, -sin_t, sin_t)

    def rope(t):
        partner = jnp.where(first_half, pltpu.roll(t, LANES - half, axis=1),
                            pltpu.roll(t, half, axis=1))
        return t * cos_t + partner * sin_t

    def project(col, width):
        return jnp.dot(hb, w_ref[:, col:col + width], preferred_element_type=F32)

    scale = HEAD_DIM ** -0.5
    col = 0
    for out_ref, width, rotary, mult in (
            (qa_ref, MIX_WIDTH, True, scale), (ka_ref, KV_A_WIDTH, True, None),
            (va_ref, KV_A_WIDTH, False, None), (qb_ref, MIX_WIDTH, True, scale),
            (kb_ref, MIX_WIDTH, True, None), (vb_ref, MIX_WIDTH, False, None)):
        t = project(col, width)
        for g in range(width // LANES):
            tg = t[:, g * LANES:(g + 1) * LANES]
            if rotary:
                tg = rope(tg)
            if mult is not None:
                tg = tg * mult
            out_ref[0, :, g * LANES:(g + 1) * LANES] = tg.astype(BF16)
        col += width


def _project(x, mod, layer, gain, cos, sin, w_in, tm=512):
    b, s, d = x.shape
    nf = ROPE_DIM // 2
    tok = lambda w: pl.BlockSpec((1, tm, w), lambda i, j: (i, j, 0))
    widths = (MIX_WIDTH, KV_A_WIDTH, KV_A_WIDTH, MIX_WIDTH, MIX_WIDTH, MIX_WIDTH)
    return pl.pallas_call(
        _proj_kernel,
        grid=(b, s // tm),
        in_specs=[
            tok(d),
            pl.BlockSpec((1, 1, N_MOD, d), lambda i, j: (layer, i, 0, 0)),
            pl.BlockSpec((None, 1, d), lambda i, j: (layer, 0, 0)),
            tok(nf), tok(nf),
            _resident((None, d, D_IN), lambda i, j: (layer, 0, 0)),
        ],
        out_specs=[tok(w) for w in widths],
        out_shape=[jax.ShapeDtypeStruct((b, s, w), BF16) for w in widths],
        compiler_params=_params(2),
        name="in_proj_rope",
    )(x, mod, gain, cos, sin, w_in)


def _fill_window(buf, prev_ref, cur_ref, next_ref, hw, m):
    buf[0:hw] = prev_ref[0]
    buf[hw:hw + m] = cur_ref[0]
    buf[hw + m:hw + m + hw] = next_ref[0]


def _band_mask(key_start, hw, n_keys, seq_len):
    jj = lax.broadcasted_iota(jnp.int32, (Q_ROWS, n_keys), 1)
    ii = lax.broadcasted_iota(jnp.int32, (Q_ROWS, n_keys), 0)
    rel = jj - ii
    return (rel >= 0) & (rel <= 2 * hw) & (jj >= -key_start) & (jj < seq_len - key_start)


def _attend_pair(q_pair, k_win, v_win, mask, sinks):
    lane = lax.broadcasted_iota(jnp.int32, q_pair.shape, 1)
    is_lo = lane < HEAD_DIM
    zero = jnp.zeros_like(q_pair)
    outs, lses = [], []
    for q_head, sink in ((jnp.where(is_lo, q_pair, zero), sinks[0]),
                         (jnp.where(is_lo, zero, q_pair), sinks[1])):
        s = lax.dot_general(q_head, k_win, (((1,), (1,)), ((), ())), preferred_element_type=F32)
        s = jnp.where(mask, s, NEG_INF)
        m = jnp.max(s, axis=1, keepdims=True)
        if sink is not None:
            m = jnp.maximum(m, sink)
        e = jnp.exp(s - m)
        l_keys = jnp.sum(e, axis=1, keepdims=True)
        denom = l_keys if sink is None else l_keys + jnp.exp(sink - m)
        pv = jnp.dot(e.astype(BF16), v_win, preferred_element_type=F32)
        outs.append(pv / denom)
        lses.append(m + jnp.log(l_keys))
    return jnp.where(is_lo, outs[0], outs[1]), lses


def _attn_a_kernel(sink_ref, q_ref, kp_ref, kc_ref, kn_ref, vp_ref, vc_ref, vn_ref, g_ref,
                   y_ref, kbuf, vbuf, *, layer, m, seq_len):
    hw = A_HALF_WINDOW
    n_keys = Q_ROWS + 2 * hw
    _fill_window(kbuf, kp_ref, kc_ref, kn_ref, hw, m)
    _fill_window(vbuf, vp_ref, vc_ref, vn_ref, hw, m)
    row0 = pl.program_id(1) * m
    n_pairs = MIX_WIDTH // LANES

    def body(sb, carry):
        r0 = pl.multiple_of(sb * Q_ROWS, Q_ROWS)
        mask = _band_mask(row0 + r0 - hw, hw, n_keys, seq_len)
        k_win = kbuf[pl.ds(r0, n_keys), :]
        v_win = vbuf[pl.ds(r0, n_keys), :]
        outs = []
        for p in range(n_pairs):
            sinks = (sink_ref[layer, p], sink_ref[layer, p + n_pairs])
            o, _ = _attend_pair(q_ref[0, pl.ds(r0, Q_ROWS), p * LANES:(p + 1) * LANES],
                                k_win, v_win, mask, sinks)
            outs.append(o)
        ssq = sum(jnp.sum(o * o, axis=1, keepdims=True) for o in outs)
        inv = lax.rsqrt(ssq / MIX_WIDTH + EPS)
        for p, o in enumerate(outs):
            y = (o * inv) * g_ref[:, p * LANES:(p + 1) * LANES]
            y_ref[0, pl.ds(r0, Q_ROWS), p * LANES:(p + 1) * LANES] = y.astype(BF16)
        return carry

    lax.fori_loop(0, m // Q_ROWS, body, 0)


def _attention_a(qa, ka, va, sink, gain, layer, m=512):
    b, s, _ = qa.shape
    hw = A_HALF_WINDOW
    per = m // hw
    last = s // hw - 1
    cur = lambda w: pl.BlockSpec((1, m, w), lambda i, j: (i, j, 0))
    prev = pl.BlockSpec((1, hw, KV_A_WIDTH), lambda i, j: (i, jnp.maximum(j * per - 1, 0), 0))
    nxt = pl.BlockSpec((1, hw, KV_A_WIDTH), lambda i, j: (i, jnp.minimum((j + 1) * per, last), 0))
    kern = functools.partial(_attn_a_kernel, layer=layer, m=m, seq_len=s)
    return pl.pallas_call(
        kern,
        grid=(b, s // m),
        in_specs=[
            pl.BlockSpec(memory_space=pltpu.SMEM),
            cur(MIX_WIDTH),
            prev, cur(KV_A_WIDTH), nxt, prev, cur(KV_A_WIDTH), nxt,
            pl.BlockSpec((None, 1, MIX_WIDTH), lambda i, j: (layer, 0, 0)),
        ],
        out_specs=cur(MIX_WIDTH),
        out_shape=jax.ShapeDtypeStruct((b, s, MIX_WIDTH), BF16),
        scratch_shapes=[pltpu.VMEM((m + 2 * hw, KV_A_WIDTH), BF16)] * 2,
        compiler_params=_params(2),
        name="attn_a",
    )(sink, qa, ka, ka, ka, va, va, va, gain)


def _attn_b_kernel(q_ref, kp_ref, kc_ref, kn_ref, vp_ref, vc_ref, vn_ref,
                   o_ref, lse_ref, kbuf, vbuf, *, hw, m, seq_len):
    n_keys = Q_ROWS + 2 * hw
    _fill_window(kbuf, kp_ref, kc_ref, kn_ref, hw, m)
    _fill_window(vbuf, vp_ref, vc_ref, vn_ref, hw, m)
    row0 = pl.program_id(2) * m
    head_of_lane = lax.broadcasted_iota(jnp.int32, (Q_ROWS, LANES), 1) >> 4

    def body(sb, carry):
        r0 = pl.multiple_of(sb * Q_ROWS, Q_ROWS)
        mask = _band_mask(row0 + r0 - hw, hw, n_keys, seq_len)
        lse_tile = jnp.zeros((Q_ROWS, LANES), F32)
        for p in range(MIX_WIDTH // LANES):
            cols = slice(p * LANES, (p + 1) * LANES)
            o, lses = _attend_pair(q_ref[0, pl.ds(r0, Q_ROWS), cols],
                                   kbuf[pl.ds(r0, n_keys), cols],
                                   vbuf[pl.ds(r0, n_keys), cols], mask, (None, None))
            o_ref[0, pl.ds(r0, Q_ROWS), cols] = o
            lse_tile = jnp.where(head_of_lane == 2 * p, lses[0], lse_tile)
            lse_tile = jnp.where(head_of_lane == 2 * p + 1, lses[1], lse_tile)
        lse_ref[0, pl.ds(r0, Q_ROWS), :] = lse_tile
        return carry

    lax.fori_loop(0, m // Q_ROWS, body, 0)


def _attention_b_branch(qb, kb, vb, window, dilation):
    b, s, w = qb.shape
    seq = s // dilation
    hw = window // (2 * dilation)
    m = min(seq, 512)
    per = m // hw
    last = seq // hw - 1
    view = lambda t: t.reshape(b, seq, dilation * w)
    cur = lambda width: pl.BlockSpec((1, m, width), lambda i, r, j: (i, j, r))
    prev = pl.BlockSpec((1, hw, w), lambda i, r, j: (i, jnp.maximum(j * per - 1, 0), r))
    nxt = pl.BlockSpec((1, hw, w), lambda i, r, j: (i, jnp.minimum((j + 1) * per, last), r))
    kern = functools.partial(_attn_b_kernel, hw=hw, m=m, seq_len=seq)
    q, k, v = view(qb), view(kb), view(vb)
    o, lse = pl.pallas_call(
        kern,
        grid=(b, dilation, seq // m),
        in_specs=[cur(w), prev, cur(w), nxt, prev, cur(w), nxt],
        out_specs=[cur(w), cur(LANES)],
        out_shape=[jax.ShapeDtypeStruct((b, seq, dilation * w), F32),
                   jax.ShapeDtypeStruct((b, seq, dilation * LANES), F32)],
        scratch_shapes=[pltpu.VMEM((m + 2 * hw, w), BF16)] * 2,
        compiler_params=_params(3),
        name=f"attn_b_d{dilation}",
    )(q, k, k, k, v, v, v)
    return o.reshape(b, s, w), lse.reshape(b, s, LANES)


def _out_kernel(x_ref, mod_ref, ya_ref, o1_ref, o2_ref, o3_ref, l1_ref, l2_ref, l3_ref,
                gb_ref, w_ref, o_ref):
    x = x_ref[0]
    tm = x.shape[0]
    lses = [r[0] for r in (l1_ref, l2_ref, l3_ref)]
    mx = jnp.maximum(jnp.maximum(lses[0], lses[1]), lses[2])
    es = [jnp.exp(l - mx) for l in lses]
    den = es[0] + es[1] + es[2]
    wts = [e / den for e in es]
    is_lo = lax.broadcasted_iota(jnp.int32, (tm, LANES), 1) < HEAD_DIM
    per_head = LANES // N_HEADS
    branches = (o1_ref, o2_ref, o3_ref)
    groups = []
    for p in range(MIX_WIDTH // LANES):
        cols = slice(p * LANES, (p + 1) * LANES)
        acc = None
        for wt, o_r in zip(wts, branches):
            lo = wt[:, 2 * p * per_head:2 * p * per_head + 1]
            hi = wt[:, (2 * p + 1) * per_head:(2 * p + 1) * per_head + 1]
            term = jnp.where(is_lo, lo, hi) * o_r[0, :, cols]
            acc = term if acc is None else acc + term
        groups.append(acc)
    ssq = sum(jnp.sum(g * g, axis=1, keepdims=True) for g in groups)
    inv = lax.rsqrt(ssq / MIX_WIDTH + EPS)
    yb = jnp.concatenate(
        [((g * inv) * gb_ref[:, p * LANES:(p + 1) * LANES]).astype(BF16)
         for p, g in enumerate(groups)], axis=1)
    mix = jnp.dot(ya_ref[0], w_ref[0:MIX_WIDTH, :], preferred_element_type=F32)
    mix = mix + jnp.dot(yb, w_ref[MIX_WIDTH:2 * MIX_WIDTH, :], preferred_element_type=F32)
    o_ref[0] = x + mod_ref[0, 0, 5:6, :] * mix


def _merge_project(x, mod, layer, ya, outs, lses, gain_b, w_out, tm=512):
    b, s, d = x.shape
    tok = lambda w: pl.BlockSpec((1, tm, w), lambda i, j: (i, j, 0))
    return pl.pallas_call(
        _out_kernel,
        grid=(b, s // tm),
        in_specs=[
            tok(d),
            pl.BlockSpec((1, 1, N_MOD, d), lambda i, j: (layer, i, 0, 0)),
            tok(MIX_WIDTH), tok(MIX_WIDTH), tok(MIX_WIDTH), tok(MIX_WIDTH),
            tok(LANES), tok(LANES), tok(LANES),
            pl.BlockSpec((None, 1, MIX_WIDTH), lambda i, j: (layer, 0, 0)),
            _resident((None, 2 * MIX_WIDTH, d), lambda i, j: (layer, 0, 0)),
        ],
        out_specs=tok(d),
        out_shape=jax.ShapeDtypeStruct(x.shape, F32),
        compiler_params=_params(2),
        name="merge_out_proj",
    )(x, mod, ya, *outs, *lses, gain_b, w_out)


def _pair_kv_heads(n_heads):
    half = n_heads // 2
    return [h for p in range(half) for h in (p, p + half)]


def kernel(x, c, positions, ada_w, ada_b, norm_ffn1, ffn1_wi, ffn1_wo, norm_mix, w_in, sink,
           onorm_a, onorm_b, w_out, norm_ffn2, ffn2_wi, ffn2_wo, final_norm):
    depth = ada_w.shape[0]
    order = _pair_kv_heads(N_HEADS)
    cols_a = np.concatenate([np.arange(h * HEAD_DIM, (h + 1) * HEAD_DIM) for h in order])
    perm_in = np.concatenate([cols_a, np.arange(MIX_WIDTH, D_IN)])
    perm_out = np.concatenate([cols_a, np.arange(MIX_WIDTH, 2 * MIX_WIDTH)])
    w_in_b = w_in[:, :, perm_in].astype(BF16)
    w_out_b = w_out[:, perm_out, :].astype(BF16)
    gain_a = onorm_a[:, cols_a].reshape(depth, 1, MIX_WIDTH)
    gain_b = onorm_b.reshape(depth, 1, MIX_WIDTH)
    sink_p = sink[:, np.array(order)]
    wi1, wo1 = ffn1_wi.astype(BF16), ffn1_wo.astype(BF16)
    wi2, wo2 = ffn2_wi.astype(BF16), ffn2_wo.astype(BF16)
    g_ffn1 = norm_ffn1.reshape(depth, 1, D_MODEL)
    g_mix = norm_mix.reshape(depth, 1, D_MODEL)
    g_ffn2 = norm_ffn2.reshape(depth, 1, D_MODEL)
    g_final = final_norm.reshape(1, D_MODEL)

    mod = _modulation(c, ada_w, ada_b)
    cos, sin = _rope_tables(positions)
    for l in range(depth):
        x = _ffn(x, mod, l, 0, g_ffn1, wi1, wo1, g_final, False)
        qa, ka, va, qb, kb, vb = _project(x, mod, l, g_mix, cos, sin, w_in_b)
        ya = _attention_a(qa, ka, va, sink_p, gain_a, l)
        outs, lses = zip(*[_attention_b_branch(qb, kb, vb, w, d) for w, d in B_BRANCHES])
        x = _merge_project(x, mod, l, ya, outs, lses, gain_b, w_out_b)
        x = _ffn(x, mod, l, 6, g_ffn2, wi2, wo2, g_final, l == depth - 1)
    return x
```

```python
import functools

import numpy as np
import jax
import jax.numpy as jnp
from jax import lax
from jax.experimental import pallas as pl
from jax.experimental.pallas import tpu as pltpu

D_MODEL = 1024
HEAD_DIM = 64
N_HEADS = 8
MIX_WIDTH = N_HEADS * HEAD_DIM
KV_A_WIDTH = 2 * HEAD_DIM
A_HALF_WINDOW = 128
B_BRANCHES = ((128, 1), (512, 4), (2048, 16))
DILATIONS = tuple(d for _, d in B_BRANCHES)
ROPE_THETA = 500000.0
ROPE_DIM = 16
D_FF = 2816
N_MOD = 9
D_IN = 2304
EPS = 1e-6
NEG_INF = -1e30

LANES = 128
Q_ROWS = 128
F_CHUNK = 256
VMEM_LIMIT = 56 * 1024 * 1024

F32 = jnp.float32
BF16 = jnp.bfloat16


def _params(n_axes):
    return pltpu.CompilerParams(
        dimension_semantics=("arbitrary",) * n_axes, vmem_limit_bytes=VMEM_LIMIT)


def _resident(block_shape, index_map):
    return pl.BlockSpec(block_shape, index_map, pipeline_mode=pl.Buffered(1))


def _norm_mod(x, gain, shift, scale):
    y = x * lax.rsqrt(jnp.mean(x * x, axis=-1, keepdims=True) + EPS)
    return (y * gain) * (1.0 + scale) + shift


def _mod_kernel(c_ref, w_ref, b_ref, o_ref):
    ca = jax.nn.silu(c_ref[...]).astype(BF16)
    o_ref[0] = jnp.dot(ca, w_ref[0].astype(BF16), preferred_element_type=F32) + b_ref[0]


def _modulation(c, ada_w, ada_b):
    depth, d, n = ada_w.shape
    b = c.shape[0]
    tn = n // 8
    out = pl.pallas_call(
        _mod_kernel,
        grid=(depth, n // tn),
        in_specs=[
            pl.BlockSpec((b, d), lambda l, j: (0, 0)),
            pl.BlockSpec((1, d, tn), lambda l, j: (l, 0, j)),
            pl.BlockSpec((1, 1, tn), lambda l, j: (l, 0, j)),
        ],
        out_specs=pl.BlockSpec((1, b, tn), lambda l, j: (l, 0, j)),
        out_shape=jax.ShapeDtypeStruct((depth, b, n), F32),
        compiler_params=_params(2),
        name="adaln_mod",
    )(c, ada_w, ada_b.reshape(depth, 1, n))
    return out.reshape(depth, b, N_MOD, d)


def _ffn_kernel(x_ref, mod_ref, g_ref, wi_ref, wo_ref, fg_ref, o_ref, act_ref, *,
                mod_row, final_norm):
    x = x_ref[0]
    shift = mod_ref[0, 0, mod_row:mod_row + 1, :]
    scale = mod_ref[0, 0, mod_row + 1:mod_row + 2, :]
    gate_res = mod_ref[0, 0, mod_row + 2:mod_row + 3, :]
    hb = _norm_mod(x, g_ref[...], shift, scale).astype(BF16)
    for j in range(D_FF // F_CHUNK):
        lo = j * F_CHUNK
        gate = jnp.dot(hb, wi_ref[:, lo:lo + F_CHUNK], preferred_element_type=F32)
        up = jnp.dot(hb, wi_ref[:, D_FF + lo:D_FF + lo + F_CHUNK], preferred_element_type=F32)
        act_ref[:, lo:lo + F_CHUNK] = (jax.nn.silu(gate) * up).astype(BF16)
    y = jnp.dot(act_ref[...], wo_ref[...], preferred_element_type=F32)
    out = x + (0.5 * gate_res) * y
    if final_norm:
        out = out * lax.rsqrt(jnp.mean(out * out, axis=-1, keepdims=True) + EPS) * fg_ref[...]
    o_ref[0] = out


def _ffn(x, mod, layer, mod_row, gain, wi, wo, final_gain, final_norm, tm=512):
    b, s, d = x.shape
    kern = functools.partial(_ffn_kernel, mod_row=mod_row, final_norm=final_norm)
    return pl.pallas_call(
        kern,
        grid=(b, s // tm),
        in_specs=[
            pl.BlockSpec((1, tm, d), lambda i, j: (i, j, 0)),
            pl.BlockSpec((1, 1, N_MOD, d), lambda i, j: (layer, i, 0, 0)),
            pl.BlockSpec((None, 1, d), lambda i, j: (layer, 0, 0)),
            _resident((None, d, 2 * D_FF), lambda i, j: (layer, 0, 0)),
            _resident((None, D_FF, d), lambda i, j: (layer, 0, 0)),
            pl.BlockSpec((1, d), lambda i, j: (0, 0)),
        ],
        out_specs=pl.BlockSpec((1, tm, d), lambda i, j: (i, j, 0)),
        out_shape=jax.ShapeDtypeStruct(x.shape, F32),
        scratch_shapes=[pltpu.VMEM((tm, D_FF), BF16)],
        compiler_params=_params(2),
        name="ffn",
    )(x, mod, gain, wi, wo, final_gain)


def _rope_cs_kernel(pos_ref, inv_ref, cos_ref, sin_ref):
    ang = pos_ref[0].astype(F32) * inv_ref[...]
    cos_ref[0] = jnp.cos(ang)
    sin_ref[0] = jnp.sin(ang)


def _rope_tables(positions):
    b, s = positions.shape
    nf = ROPE_DIM // 2
    rows = s * nf // LANES
    inv = (ROPE_THETA ** (-np.arange(0, ROPE_DIM, 2, dtype=np.float64) / ROPE_DIM)).astype(np.float32)
    inv_tile = jnp.asarray(np.tile(inv, LANES // nf).reshape(1, LANES))
    pos_rep = jnp.repeat(positions, nf, axis=1).reshape(b, rows, LANES)
    spec = pl.BlockSpec((1, rows, LANES), lambda i: (i, 0, 0))
    cos, sin = pl.pallas_call(
        _rope_cs_kernel,
        grid=(b,),
        in_specs=[spec, pl.BlockSpec((1, LANES), lambda i: (0, 0))],
        out_specs=[spec, spec],
        out_shape=[jax.ShapeDtypeStruct((b, rows, LANES), F32)] * 2,
        compiler_params=_params(1),
        name="rope_cos_sin",
    )(pos_rep, inv_tile)
    return cos.reshape(b, s, nf), sin.reshape(b, s, nf)


def _proj_kernel(x_ref, mod_ref, g_ref, cos_ref, sin_ref, w_ref, *refs):
    qa_ref, ka_ref, va_ref = refs[:3]
    n_dil = len(DILATIONS)
    b_refs = [refs[3 + t * n_dil:3 + (t + 1) * n_dil] for t in range(3)]
    relay = refs[3 + 3 * n_dil]
    x = x_ref[0]
    tm = x.shape[0]
    hb = _norm_mod(x, g_ref[...], mod_ref[0, 0, 3:4, :], mod_ref[0, 0, 4:5, :]).astype(BF16)

    lane = lax.broadcasted_iota(jnp.int32, (tm, LANES), 1)
    in_head = lane & (HEAD_DIM - 1)
    half = ROPE_DIM // 2
    cos8 = cos_ref[0]
    sin8 = sin_ref[0]
    cos_t = jnp.ones((tm, LANES), F32)
    sin_t = jnp.zeros((tm, LANES), F32)
    for f in range(half):
        sel = (in_head < ROPE_DIM) & ((in_head & (half - 1)) == f)
        cos_t = jnp.where(sel, cos8[:, f:f + 1], cos_t)
        sin_t = jnp.where(sel, sin8[:, f:f + 1], sin_t)
    first_half = in_head < half
    sin_t = jnp.where(first_half, -sin_t, sin_t)

    def rope(t):
        partner = jnp.where(first_half, pltpu.roll(t, LANES - half, axis=1),
                            pltpu.roll(t, half, axis=1))
        return t * cos_t + partner * sin_t

    def lane_groups(col, width, rotary, mult):
        t = jnp.dot(hb, w_ref[:, col:col + width], preferred_element_type=F32)
        for g in range(width // LANES):
            tg = t[:, g * LANES:(g + 1) * LANES]
            if rotary:
                tg = rope(tg)
            if mult is not None:
                tg = tg * mult
            yield g, tg

    scale = HEAD_DIM ** -0.5
    col = 0
    for out_ref, width, rotary, mult in ((qa_ref, MIX_WIDTH, True, scale),
                                         (ka_ref, KV_A_WIDTH, True, None),
                                         (va_ref, KV_A_WIDTH, False, None)):
        for g, tg in lane_groups(col, width, rotary, mult):
            out_ref[0, :, g * LANES:(g + 1) * LANES] = tg.astype(BF16)
        col += width
    for outs, rotary, mult in ((b_refs[0], True, scale), (b_refs[1], True, None),
                               (b_refs[2], False, None)):
        for g, tg in lane_groups(col, MIX_WIDTH, rotary, mult):
            relay[g] = tg
        for out_ref, dil in zip(outs, DILATIONS):
            n = tm // dil
            for r in range(dil):
                for g in range(MIX_WIDTH // LANES):
                    rows = relay[g] if dil == 1 else relay[g, pl.ds(r, n, stride=dil), :]
                    out_ref[0, r, :, g * LANES:(g + 1) * LANES] = rows.astype(BF16)
        col += MIX_WIDTH


def _project(x, mod, layer, gain, cos, sin, w_in, tm=512):
    b, s, d = x.shape
    nf = ROPE_DIM // 2
    tok = lambda w: pl.BlockSpec((1, tm, w), lambda i, j: (i, j, 0))
    res = lambda dil: pl.BlockSpec((1, dil, tm // dil, MIX_WIDTH), lambda i, j: (i, 0, j, 0))
    a_widths = (MIX_WIDTH, KV_A_WIDTH, KV_A_WIDTH)
    out_specs = [tok(w) for w in a_widths] + [res(dil) for _ in range(3) for dil in DILATIONS]
    out_shape = ([jax.ShapeDtypeStruct((b, s, w), BF16) for w in a_widths]
                 + [jax.ShapeDtypeStruct((b, dil, s // dil, MIX_WIDTH), BF16)
                    for _ in range(3) for dil in DILATIONS])
    outs = pl.pallas_call(
        _proj_kernel,
        grid=(b, s // tm),
        in_specs=[
            tok(d),
            pl.BlockSpec((1, 1, N_MOD, d), lambda i, j: (layer, i, 0, 0)),
            pl.BlockSpec((None, 1, d), lambda i, j: (layer, 0, 0)),
            tok(nf), tok(nf),
            _resident((None, d, D_IN), lambda i, j: (layer, 0, 0)),
        ],
        out_specs=out_specs,
        out_shape=out_shape,
        scratch_shapes=[pltpu.VMEM((MIX_WIDTH // LANES, tm, LANES), F32)],
        compiler_params=_params(2),
        name="in_proj_rope",
    )(x, mod, gain, cos, sin, w_in)
    n_dil = len(DILATIONS)
    qa, ka, va = outs[:3]
    qb, kb, vb = (outs[3 + t * n_dil:3 + (t + 1) * n_dil] for t in range(3))
    return qa, ka, va, qb, kb, vb


def _fill_window(buf, prev_ref, cur_ref, next_ref, hw, m):
    buf[0:hw] = prev_ref[...]
    buf[hw:hw + m] = cur_ref[...]
    buf[hw + m:hw + m + hw] = next_ref[...]


def _band_mask(key_start, hw, n_keys, seq_len):
    jj = lax.broadcasted_iota(jnp.int32, (Q_ROWS, n_keys), 1)
    ii = lax.broadcasted_iota(jnp.int32, (Q_ROWS, n_keys), 0)
    rel = jj - ii
    return (rel >= 0) & (rel <= 2 * hw) & (jj >= -key_start) & (jj < seq_len - key_start)


def _attend_pair(q_pair, k_win, v_win, mask, sinks):
    lane = lax.broadcasted_iota(jnp.int32, q_pair.shape, 1)
    is_lo = lane < HEAD_DIM
    zero = jnp.zeros_like(q_pair)
    outs, lses = [], []
    for q_head, sink in ((jnp.where(is_lo, q_pair, zero), sinks[0]),
                         (jnp.where(is_lo, zero, q_pair), sinks[1])):
        s = lax.dot_general(q_head, k_win, (((1,), (1,)), ((), ())), preferred_element_type=F32)
        s = jnp.where(mask, s, NEG_INF)
        m = jnp.max(s, axis=1, keepdims=True)
        if sink is not None:
            m = jnp.maximum(m, sink)
        e = jnp.exp(s - m)
        l_keys = jnp.sum(e, axis=1, keepdims=True)
        denom = l_keys if sink is None else l_keys + jnp.exp(sink - m)
        pv = jnp.dot(e.astype(BF16), v_win, preferred_element_type=F32)
        outs.append(pv / denom)
        lses.append(m + jnp.log(l_keys))
    return jnp.where(is_lo, outs[0], outs[1]), lses


def _attn_a_kernel(sink_ref, q_ref, kp_ref, kc_ref, kn_ref, vp_ref, vc_ref, vn_ref, g_ref,
                   y_ref, kbuf, vbuf, *, layer, m, seq_len):
    hw = A_HALF_WINDOW
    n_keys = Q_ROWS + 2 * hw
    _fill_window(kbuf, kp_ref, kc_ref, kn_ref, hw, m)
    _fill_window(vbuf, vp_ref, vc_ref, vn_ref, hw, m)
    row0 = pl.program_id(1) * m
    n_pairs = MIX_WIDTH // LANES

    def body(sb, carry):
        r0 = pl.multiple_of(sb * Q_ROWS, Q_ROWS)
        mask = _band_mask(row0 + r0 - hw, hw, n_keys, seq_len)
        k_win = kbuf[pl.ds(r0, n_keys), :]
        v_win = vbuf[pl.ds(r0, n_keys), :]
        outs = []
        for p in range(n_pairs):
            sinks = (sink_ref[layer, p], sink_ref[layer, p + n_pairs])
            o, _ = _attend_pair(q_ref[pl.ds(r0, Q_ROWS), p * LANES:(p + 1) * LANES],
                                k_win, v_win, mask, sinks)
            outs.append(o)
        ssq = sum(jnp.sum(o * o, axis=1, keepdims=True) for o in outs)
        inv = lax.rsqrt(ssq / MIX_WIDTH + EPS)
        for p, o in enumerate(outs):
            y = (o * inv) * g_ref[:, p * LANES:(p + 1) * LANES]
            y_ref[pl.ds(r0, Q_ROWS), p * LANES:(p + 1) * LANES] = y.astype(BF16)
        return carry

    lax.fori_loop(0, m // Q_ROWS, body, 0)


def _attention_a(qa, ka, va, sink, gain, layer, m=512):
    b, s, _ = qa.shape
    hw = A_HALF_WINDOW
    per = m // hw
    last = s // hw - 1
    cur = lambda w: pl.BlockSpec((None, m, w), lambda i, j: (i, j, 0))
    prev = pl.BlockSpec((None, hw, KV_A_WIDTH), lambda i, j: (i, jnp.maximum(j * per - 1, 0), 0))
    nxt = pl.BlockSpec((None, hw, KV_A_WIDTH),
                       lambda i, j: (i, jnp.minimum((j + 1) * per, last), 0))
    kern = functools.partial(_attn_a_kernel, layer=layer, m=m, seq_len=s)
    return pl.pallas_call(
        kern,
        grid=(b, s // m),
        in_specs=[
            pl.BlockSpec(memory_space=pltpu.SMEM),
            cur(MIX_WIDTH),
            prev, cur(KV_A_WIDTH), nxt, prev, cur(KV_A_WIDTH), nxt,
            pl.BlockSpec((None, 1, MIX_WIDTH), lambda i, j: (layer, 0, 0)),
        ],
        out_specs=cur(MIX_WIDTH),
        out_shape=jax.ShapeDtypeStruct((b, s, MIX_WIDTH), BF16),
        scratch_shapes=[pltpu.VMEM((m + 2 * hw, KV_A_WIDTH), BF16)] * 2,
        compiler_params=_params(2),
        name="attn_a",
    )(sink, qa, ka, ka, ka, va, va, va, gain)


def _attn_b_kernel(q_ref, kp_ref, kc_ref, kn_ref, vp_ref, vc_ref, vn_ref,
                   o_ref, lse_ref, kbuf, vbuf, *, hw, m, seq_len):
    n_keys = Q_ROWS + 2 * hw
    _fill_window(kbuf, kp_ref, kc_ref, kn_ref, hw, m)
    _fill_window(vbuf, vp_ref, vc_ref, vn_ref, hw, m)
    row0 = pl.program_id(2) * m
    head_of_lane = lax.broadcasted_iota(jnp.int32, (Q_ROWS, LANES), 1) >> 4

    def body(sb, carry):
        r0 = pl.multiple_of(sb * Q_ROWS, Q_ROWS)
        mask = _band_mask(row0 + r0 - hw, hw, n_keys, seq_len)
        lse_tile = jnp.zeros((Q_ROWS, LANES), F32)
        for p in range(MIX_WIDTH // LANES):
            cols = slice(p * LANES, (p + 1) * LANES)
            o, lses = _attend_pair(q_ref[pl.ds(r0, Q_ROWS), cols],
                                   kbuf[pl.ds(r0, n_keys), cols],
                                   vbuf[pl.ds(r0, n_keys), cols], mask, (None, None))
            o_ref[pl.ds(r0, Q_ROWS), cols] = o
            lse_tile = jnp.where(head_of_lane == 2 * p, lses[0], lse_tile)
            lse_tile = jnp.where(head_of_lane == 2 * p + 1, lses[1], lse_tile)
        lse_ref[pl.ds(r0, Q_ROWS), :] = lse_tile
        return carry

    lax.fori_loop(0, m // Q_ROWS, body, 0)


def _attention_b_branch(q, k, v, window):
    b, dilation, seq, w = q.shape
    hw = window // (2 * dilation)
    m = min(seq, 512)
    per = m // hw
    last = seq // hw - 1
    cur = lambda width: pl.BlockSpec((None, None, m, width), lambda i, r, j: (i, r, j, 0))
    prev = pl.BlockSpec((None, None, hw, w),
                        lambda i, r, j: (i, r, jnp.maximum(j * per - 1, 0), 0))
    nxt = pl.BlockSpec((None, None, hw, w),
                       lambda i, r, j: (i, r, jnp.minimum((j + 1) * per, last), 0))
    kern = functools.partial(_attn_b_kernel, hw=hw, m=m, seq_len=seq)
    return pl.pallas_call(
        kern,
        grid=(b, dilation, seq // m),
        in_specs=[cur(w), prev, cur(w), nxt, prev, cur(w), nxt],
        out_specs=[cur(w), cur(LANES)],
        out_shape=[jax.ShapeDtypeStruct((b, dilation, seq, w), F32),
                   jax.ShapeDtypeStruct((b, dilation, seq, LANES), F32)],
        scratch_shapes=[pltpu.VMEM((m + 2 * hw, w), BF16)] * 2,
        compiler_params=_params(3),
        name=f"attn_b_d{dilation}",
    )(q, k, k, k, v, v, v)


def _out_kernel(x_ref, mod_ref, ya_ref, o1_ref, o2_ref, o3_ref, l1_ref, l2_ref, l3_ref,
                gb_ref, w_ref, o_ref, relay_o, relay_l):
    x = x_ref[0]
    tm = x.shape[0]
    n_groups = MIX_WIDTH // LANES

    def natural_order(src_ref, dil, dst, n_lane_groups):
        n = tm // dil
        for r in range(dil):
            for g in range(n_lane_groups):
                dst[g, pl.ds(r, n, stride=dil), :] = src_ref[r, :, g * LANES:(g + 1) * LANES]

    lses = []
    for i, (l_ref, dil) in enumerate(zip((l1_ref, l2_ref, l3_ref), DILATIONS)):
        if dil == 1:
            lses.append(l_ref[0])
        else:
            natural_order(l_ref, dil, relay_l.at[i], 1)
            lses.append(relay_l[i, 0])
    mx = jnp.maximum(jnp.maximum(lses[0], lses[1]), lses[2])
    es = [jnp.exp(l - mx) for l in lses]
    den = es[0] + es[1] + es[2]
    wts = [e / den for e in es]
    is_lo = lax.broadcasted_iota(jnp.int32, (tm, LANES), 1) < HEAD_DIM
    per_head = LANES // N_HEADS
    groups = [None] * n_groups
    for wt, o_r, dil in zip(wts, (o1_ref, o2_ref, o3_ref), DILATIONS):
        if dil != 1:
            natural_order(o_r, dil, relay_o, n_groups)
        for p in range(n_groups):
            vals = o_r[0, :, p * LANES:(p + 1) * LANES] if dil == 1 else relay_o[p]
            lo = wt[:, 2 * p * per_head:2 * p * per_head + 1]
            hi = wt[:, (2 * p + 1) * per_head:(2 * p + 1) * per_head + 1]
            term = jnp.where(is_lo, lo, hi) * vals
            groups[p] = term if groups[p] is None else groups[p] + term
    ssq = sum(jnp.sum(g * g, axis=1, keepdims=True) for g in groups)
    inv = lax.rsqrt(ssq / MIX_WIDTH + EPS)
    yb = jnp.concatenate(
        [((g * inv) * gb_ref[:, p * LANES:(p + 1) * LANES]).astype(BF16)
         for p, g in enumerate(groups)], axis=1)
    mix = jnp.dot(ya_ref[0], w_ref[0:MIX_WIDTH, :], preferred_element_type=F32)
    mix = mix + jnp.dot(yb, w_ref[MIX_WIDTH:2 * MIX_WIDTH, :], preferred_element_type=F32)
    o_ref[0] = x + mod_ref[0, 0, 5:6, :] * mix


def _merge_project(x, mod, layer, ya, outs, lses, gain_b, w_out, tm=512):
    b, s, d = x.shape
    tok = lambda w: pl.BlockSpec((1, tm, w), lambda i, j: (i, j, 0))
    res = lambda dil, w: pl.BlockSpec((None, dil, tm // dil, w), lambda i, j: (i, 0, j, 0))
    return pl.pallas_call(
        _out_kernel,
        grid=(b, s // tm),
        in_specs=[
            tok(d),
            pl.BlockSpec((1, 1, N_MOD, d), lambda i, j: (layer, i, 0, 0)),
            tok(MIX_WIDTH),
            *[res(dil, MIX_WIDTH) for dil in DILATIONS],
            *[res(dil, LANES) for dil in DILATIONS],
            pl.BlockSpec((None, 1, MIX_WIDTH), lambda i, j: (layer, 0, 0)),
            _resident((None, 2 * MIX_WIDTH, d), lambda i, j: (layer, 0, 0)),
        ],
        out_specs=tok(d),
        out_shape=jax.ShapeDtypeStruct(x.shape, F32),
        scratch_shapes=[pltpu.VMEM((MIX_WIDTH // LANES, tm, LANES), F32),
                        pltpu.VMEM((len(DILATIONS), 1, tm, LANES), F32)],
        compiler_params=_params(2),
        name="merge_out_proj",
    )(x, mod, ya, *outs, *lses, gain_b, w_out)


def _pair_kv_heads(n_heads):
    half = n_heads // 2
    return [h for p in range(half) for h in (p, p + half)]


def kernel(x, c, positions, ada_w, ada_b, norm_ffn1, ffn1_wi, ffn1_wo, norm_mix, w_in, sink,
           onorm_a, onorm_b, w_out, norm_ffn2, ffn2_wi, ffn2_wo, final_norm):
    depth = ada_w.shape[0]
    order = _pair_kv_heads(N_HEADS)
    cols_a = np.concatenate([np.arange(h * HEAD_DIM, (h + 1) * HEAD_DIM) for h in order])
    perm_in = np.concatenate([cols_a, np.arange(MIX_WIDTH, D_IN)])
    perm_out = np.concatenate([cols_a, np.arange(MIX_WIDTH, 2 * MIX_WIDTH)])
    w_in_b = w_in[:, :, perm_in].astype(BF16)
    w_out_b = w_out[:, perm_out, :].astype(BF16)
    gain_a = onorm_a[:, cols_a].reshape(depth, 1, MIX_WIDTH)
    gain_b = onorm_b.reshape(depth, 1, MIX_WIDTH)
    sink_p = sink[:, np.array(order)]
    wi1, wo1 = ffn1_wi.astype(BF16), ffn1_wo.astype(BF16)
    wi2, wo2 = ffn2_wi.astype(BF16), ffn2_wo.astype(BF16)
    g_ffn1 = norm_ffn1.reshape(depth, 1, D_MODEL)
    g_mix = norm_mix.reshape(depth, 1, D_MODEL)
    g_ffn2 = norm_ffn2.reshape(depth, 1, D_MODEL)
    g_final = final_norm.reshape(1, D_MODEL)

    mod = _modulation(c, ada_w, ada_b)
    cos, sin = _rope_tables(positions)
    for l in range(depth):
        x = _ffn(x, mod, l, 0, g_ffn1, wi1, wo1, g_final, False)
        qa, ka, va, qb, kb, vb = _project(x, mod, l, g_mix, cos, sin, w_in_b)
        ya = _attention_a(qa, ka, va, sink_p, gain_a, l)
        outs, lses = zip(*[_attention_b_branch(qb[i], kb[i], vb[i], w)
                           for i, (w, _) in enumerate(B_BRANCHES)])
        x = _merge_project(x, mod, l, ya, outs, lses, gain_b, w_out_b)
        x = _ffn(x, mod, l, 6, g_ffn2, wi2, wo2, g_final, l == depth - 1)
    return x
```

```python
import functools

import numpy as np
import jax
import jax.numpy as jnp
from jax import lax
from jax.experimental import pallas as pl
from jax.experimental.pallas import tpu as pltpu

D_MODEL = 1024
HEAD_DIM = 64
N_HEADS = 8
MIX_WIDTH = N_HEADS * HEAD_DIM
KV_A_WIDTH = 2 * HEAD_DIM
A_HALF_WINDOW = 128
B_BRANCHES = ((128, 1), (512, 4), (2048, 16))
DILATIONS = tuple(d for _, d in B_BRANCHES)
ROPE_THETA = 500000.0
ROPE_DIM = 16
D_FF = 2816
N_MOD = 9
D_IN = 2304
EPS = 1e-6
NEG_INF = -1e30

LANES = 128
Q_ROWS = 128
STAGE_SKEW = 2
F_CHUNK = 256
VMEM_LIMIT = 56 * 1024 * 1024

F32 = jnp.float32
BF16 = jnp.bfloat16


def _params(n_axes):
    return pltpu.CompilerParams(
        dimension_semantics=("arbitrary",) * n_axes, vmem_limit_bytes=VMEM_LIMIT)


def _resident(block_shape, index_map):
    return pl.BlockSpec(block_shape, index_map, pipeline_mode=pl.Buffered(1))


def _norm_mod(x, gain, shift, scale):
    y = x * lax.rsqrt(jnp.mean(x * x, axis=-1, keepdims=True) + EPS)
    return (y * gain) * (1.0 + scale) + shift


def _mod_kernel(c_ref, w_ref, b_ref, o_ref):
    ca = jax.nn.silu(c_ref[...]).astype(BF16)
    o_ref[0] = jnp.dot(ca, w_ref[0].astype(BF16), preferred_element_type=F32) + b_ref[0]


def _modulation(c, ada_w, ada_b):
    depth, d, n = ada_w.shape
    b = c.shape[0]
    tn = n // 8
    out = pl.pallas_call(
        _mod_kernel,
        grid=(depth, n // tn),
        in_specs=[
            pl.BlockSpec((b, d), lambda l, j: (0, 0)),
            pl.BlockSpec((1, d, tn), lambda l, j: (l, 0, j)),
            pl.BlockSpec((1, 1, tn), lambda l, j: (l, 0, j)),
        ],
        out_specs=pl.BlockSpec((1, b, tn), lambda l, j: (l, 0, j)),
        out_shape=jax.ShapeDtypeStruct((depth, b, n), F32),
        compiler_params=_params(2),
        name="adaln_mod",
    )(c, ada_w, ada_b.reshape(depth, 1, n))
    return out.reshape(depth, b, N_MOD, d)


def _ffn_kernel(x_ref, mod_ref, g_ref, wi_ref, wo_ref, fg_ref, o_ref, act_ref, *,
                mod_row, final_norm):
    x = x_ref[0]
    shift = mod_ref[0, 0, mod_row:mod_row + 1, :]
    scale = mod_ref[0, 0, mod_row + 1:mod_row + 2, :]
    gate_res = mod_ref[0, 0, mod_row + 2:mod_row + 3, :]
    hb = _norm_mod(x, g_ref[...], shift, scale).astype(BF16)
    for j in range(D_FF // F_CHUNK):
        lo = j * F_CHUNK
        gate = jnp.dot(hb, wi_ref[:, lo:lo + F_CHUNK], preferred_element_type=F32)
        up = jnp.dot(hb, wi_ref[:, D_FF + lo:D_FF + lo + F_CHUNK], preferred_element_type=F32)
        act_ref[:, lo:lo + F_CHUNK] = (jax.nn.silu(gate) * up).astype(BF16)
    y = jnp.dot(act_ref[...], wo_ref[...], preferred_element_type=F32)
    out = x + (0.5 * gate_res) * y
    if final_norm:
        out = out * lax.rsqrt(jnp.mean(out * out, axis=-1, keepdims=True) + EPS) * fg_ref[...]
    o_ref[0] = out


def _ffn(x, mod, layer, mod_row, gain, wi, wo, final_gain, final_norm, tm=512):
    b, s, d = x.shape
    kern = functools.partial(_ffn_kernel, mod_row=mod_row, final_norm=final_norm)
    return pl.pallas_call(
        kern,
        grid=(b, s // tm),
        in_specs=[
            pl.BlockSpec((1, tm, d), lambda i, j: (i, j, 0)),
            pl.BlockSpec((1, 1, N_MOD, d), lambda i, j: (layer, i, 0, 0)),
            pl.BlockSpec((None, 1, d), lambda i, j: (layer, 0, 0)),
            _resident((None, d, 2 * D_FF), lambda i, j: (layer, 0, 0)),
            _resident((None, D_FF, d), lambda i, j: (layer, 0, 0)),
            pl.BlockSpec((1, d), lambda i, j: (0, 0)),
        ],
        out_specs=pl.BlockSpec((1, tm, d), lambda i, j: (i, j, 0)),
        out_shape=jax.ShapeDtypeStruct(x.shape, F32),
        scratch_shapes=[pltpu.VMEM((tm, D_FF), BF16)],
        compiler_params=_params(2),
        name="ffn",
    )(x, mod, gain, wi, wo, final_gain)


def _rope_cs_kernel(pos_ref, inv_ref, cos_ref, sin_ref):
    ang = pos_ref[0].astype(F32) * inv_ref[...]
    cos_ref[0] = jnp.cos(ang)
    sin_ref[0] = jnp.sin(ang)


def _rope_tables(positions):
    b, s = positions.shape
    nf = ROPE_DIM // 2
    rows = s * nf // LANES
    inv = (ROPE_THETA ** (-np.arange(0, ROPE_DIM, 2, dtype=np.float64) / ROPE_DIM)).astype(np.float32)
    inv_tile = jnp.asarray(np.tile(inv, LANES // nf).reshape(1, LANES))
    pos_rep = jnp.repeat(positions, nf, axis=1).reshape(b, rows, LANES)
    spec = pl.BlockSpec((1, rows, LANES), lambda i: (i, 0, 0))
    cos, sin = pl.pallas_call(
        _rope_cs_kernel,
        grid=(b,),
        in_specs=[spec, pl.BlockSpec((1, LANES), lambda i: (0, 0))],
        out_specs=[spec, spec],
        out_shape=[jax.ShapeDtypeStruct((b, rows, LANES), F32)] * 2,
        compiler_params=_params(1),
        name="rope_cos_sin",
    )(pos_rep, inv_tile)
    return cos.reshape(b, s, nf), sin.reshape(b, s, nf)


def _proj_kernel(x_ref, mod_ref, g_ref, cos_ref, sin_ref, w_ref, *refs):
    qa_ref, ka_ref, va_ref = refs[:3]
    n_dil = len(DILATIONS)
    b_refs = [refs[3 + t * n_dil:3 + (t + 1) * n_dil] for t in range(3)]
    relay = refs[3 + 3 * n_dil]
    x = x_ref[0]
    tm = x.shape[0]
    hb = _norm_mod(x, g_ref[...], mod_ref[0, 0, 3:4, :], mod_ref[0, 0, 4:5, :]).astype(BF16)

    lane = lax.broadcasted_iota(jnp.int32, (tm, LANES), 1)
    in_head = lane & (HEAD_DIM - 1)
    half = ROPE_DIM // 2
    cos8 = cos_ref[0]
    sin8 = sin_ref[0]
    cos_t = jnp.ones((tm, LANES), F32)
    sin_t = jnp.zeros((tm, LANES), F32)
    for f in range(half):
        sel = (in_head < ROPE_DIM) & ((in_head & (half - 1)) == f)
        cos_t = jnp.where(sel, cos8[:, f:f + 1], cos_t)
        sin_t = jnp.where(sel, sin8[:, f:f + 1], sin_t)
    first_half = in_head < half
    sin_t = jnp.where(first_half, -sin_t, sin_t)

    def rope(t):
        partner = jnp.where(first_half, pltpu.roll(t, LANES - half, axis=1),
                            pltpu.roll(t, half, axis=1))
        return t * cos_t + partner * sin_t

    def lane_groups(col, width, rotary, mult):
        t = jnp.dot(hb, w_ref[:, col:col + width], preferred_element_type=F32)
        for g in range(width // LANES):
            tg = t[:, g * LANES:(g + 1) * LANES]
            if rotary:
                tg = rope(tg)
            if mult is not None:
                tg = tg * mult
            yield g, tg

    scale = HEAD_DIM ** -0.5
    col = 0
    for out_ref, width, rotary, mult in ((qa_ref, MIX_WIDTH, True, scale),
                                         (ka_ref, KV_A_WIDTH, True, None),
                                         (va_ref, KV_A_WIDTH, False, None)):
        for g, tg in lane_groups(col, width, rotary, mult):
            out_ref[0, :, g * LANES:(g + 1) * LANES] = tg.astype(BF16)
        col += width
    for outs, rotary, mult in ((b_refs[0], True, scale), (b_refs[1], True, None),
                               (b_refs[2], False, None)):
        for g, tg in lane_groups(col, MIX_WIDTH, rotary, mult):
            relay[g] = tg
        for out_ref, dil in zip(outs, DILATIONS):
            n = tm // dil
            for r in range(dil):
                for g in range(MIX_WIDTH // LANES):
                    rows = relay[g] if dil == 1 else relay[g, pl.ds(r, n, stride=dil), :]
                    out_ref[0, r, :, g * LANES:(g + 1) * LANES] = rows.astype(BF16)
        col += MIX_WIDTH


def _project(x, mod, layer, gain, cos, sin, w_in, tm=512):
    b, s, d = x.shape
    nf = ROPE_DIM // 2
    tok = lambda w: pl.BlockSpec((1, tm, w), lambda i, j: (i, j, 0))
    res = lambda dil: pl.BlockSpec((1, dil, tm // dil, MIX_WIDTH), lambda i, j: (i, 0, j, 0))
    a_widths = (MIX_WIDTH, KV_A_WIDTH, KV_A_WIDTH)
    out_specs = [tok(w) for w in a_widths] + [res(dil) for _ in range(3) for dil in DILATIONS]
    out_shape = ([jax.ShapeDtypeStruct((b, s, w), BF16) for w in a_widths]
                 + [jax.ShapeDtypeStruct((b, dil, s // dil, MIX_WIDTH), BF16)
                    for _ in range(3) for dil in DILATIONS])
    outs = pl.pallas_call(
        _proj_kernel,
        grid=(b, s // tm),
        in_specs=[
            tok(d),
            pl.BlockSpec((1, 1, N_MOD, d), lambda i, j: (layer, i, 0, 0)),
            pl.BlockSpec((None, 1, d), lambda i, j: (layer, 0, 0)),
            tok(nf), tok(nf),
            _resident((None, d, D_IN), lambda i, j: (layer, 0, 0)),
        ],
        out_specs=out_specs,
        out_shape=out_shape,
        scratch_shapes=[pltpu.VMEM((MIX_WIDTH // LANES, tm, LANES), F32)],
        compiler_params=_params(2),
        name="in_proj_rope",
    )(x, mod, gain, cos, sin, w_in)
    n_dil = len(DILATIONS)
    qa, ka, va = outs[:3]
    qb, kb, vb = (outs[3 + t * n_dil:3 + (t + 1) * n_dil] for t in range(3))
    return qa, ka, va, qb, kb, vb


def _fill_band_bias(bias_ref, hw):
    n_keys = bias_ref.shape[1]
    jj = lax.broadcasted_iota(jnp.int32, (n_keys, 2 * Q_ROWS), 0)
    ii = lax.broadcasted_iota(jnp.int32, (n_keys, 2 * Q_ROWS), 1) & (Q_ROWS - 1)
    rel = jj - ii
    band = (rel >= 0) & (rel <= 2 * hw)
    for c, valid in enumerate((band & (jj >= hw), band, band & (jj < n_keys - hw))):
        bias_ref[c] = jnp.where(valid, 0.0, NEG_INF)


def _key_window(prev_ref, cur_ref, next_ref, sb, hw, cols):
    m = cur_ref.shape[0]
    lo, hi = sb * Q_ROWS - hw, (sb + 1) * Q_ROWS + hw
    parts = [cur_ref[max(lo, 0):min(hi, m), cols]]
    if lo < 0:
        parts.insert(0, prev_ref[:, cols])
    if hi > m:
        parts.append(next_ref[:, cols])
    return parts[0] if len(parts) == 1 else jnp.concatenate(parts, axis=0)


def _band_case(sb, n_sub, step_axis):
    if sb == 0:
        return jnp.where(pl.program_id(step_axis) == 0, 0, 1)
    if sb == n_sub - 1:
        return jnp.where(pl.program_id(step_axis) == pl.num_programs(step_axis) - 1, 2, 1)
    return 1


def _attend_groups(groups, emit):
    n = len(groups)
    state = [None] * n

    def scores(g):
        q_pair, k_win, v_win, bias, sink_row = groups[g]()
        is_lo = lax.broadcasted_iota(jnp.int32, q_pair.shape, 1) < HEAD_DIM
        zero = jnp.zeros_like(q_pair)
        q_both = jnp.concatenate(
            [jnp.where(is_lo, q_pair, zero), jnp.where(is_lo, zero, q_pair)], axis=0)
        s = lax.dot_general(k_win, q_both, (((1,), (1,)), ((), ())), preferred_element_type=F32)
        state[g] = (s + bias, v_win, sink_row)

    def softmax(g):
        s, v_win, sink_row = state[g]
        m = jnp.max(s, axis=0, keepdims=True)
        if sink_row is not None:
            m = jnp.maximum(m, sink_row)
        e = jnp.exp(s - m)
        l_keys = jnp.sum(e, axis=0, keepdims=True)
        denom = l_keys if sink_row is None else l_keys + jnp.exp(sink_row - m)
        state[g] = (e.astype(BF16), v_win, denom, m + jnp.log(l_keys))

    def values(g):
        e, v_win, denom, lse = state[g]
        pv = lax.dot_general(v_win, e, (((0,), (0,)), ((), ())), preferred_element_type=F32)
        inv = 1.0 / denom
        out_t = jnp.concatenate(
            [pv[0:HEAD_DIM, 0:Q_ROWS] * inv[:, 0:Q_ROWS],
             pv[HEAD_DIM:2 * HEAD_DIM, Q_ROWS:2 * Q_ROWS] * inv[:, Q_ROWS:]], axis=0)
        state[g] = None
        emit(g, out_t, lse)

    for t in range(n + 2 * STAGE_SKEW):
        if t < n:
            scores(t)
        if STAGE_SKEW <= t < n + STAGE_SKEW:
            softmax(t - STAGE_SKEW)
        if t >= 2 * STAGE_SKEW:
            values(t - 2 * STAGE_SKEW)


def _attn_a_kernel(sink_ref, q_ref, kp_ref, kc_ref, kn_ref, vp_ref, vc_ref, vn_ref, g_ref,
                   y_ref, bias_ref, *, layer):
    hw = A_HALF_WINDOW
    m = q_ref.shape[0]
    n_sub = m // Q_ROWS
    n_pairs = MIX_WIDTH // LANES

    @pl.when((pl.program_id(0) == 0) & (pl.program_id(1) == 0))
    def _():
        _fill_band_bias(bias_ref, hw)

    is_lo_query = lax.broadcasted_iota(jnp.int32, (1, 2 * Q_ROWS), 1) < Q_ROWS
    all_cols = slice(0, KV_A_WIDTH)

    def group(sb, p):
        def load():
            sink_row = jnp.where(is_lo_query, sink_ref[layer, p], sink_ref[layer, p + n_pairs])
            return (q_ref[sb * Q_ROWS:(sb + 1) * Q_ROWS, p * LANES:(p + 1) * LANES],
                    _key_window(kp_ref, kc_ref, kn_ref, sb, hw, all_cols),
                    _key_window(vp_ref, vc_ref, vn_ref, sb, hw, all_cols),
                    bias_ref[_band_case(sb, n_sub, 1)], sink_row)
        return load

    pending = []

    def emit(g, out_t, _):
        pending.append(out_t)
        if len(pending) < n_pairs:
            return
        sb = g // n_pairs
        ssq = sum(jnp.sum(o * o, axis=0, keepdims=True) for o in pending)
        inv = lax.rsqrt(ssq / MIX_WIDTH + EPS)
        for p, o in enumerate(pending):
            y = (o * inv).T * g_ref[:, p * LANES:(p + 1) * LANES]
            y_ref[sb * Q_ROWS:(sb + 1) * Q_ROWS, p * LANES:(p + 1) * LANES] = y.astype(BF16)
        pending.clear()

    _attend_groups([group(sb, p) for sb in range(n_sub) for p in range(n_pairs)], emit)


def _attention_a(qa, ka, va, sink, gain, layer, m=512):
    b, s, _ = qa.shape
    hw = A_HALF_WINDOW
    per = m // hw
    last = s // hw - 1
    cur = lambda w: pl.BlockSpec((None, m, w), lambda i, j: (i, j, 0))
    prev = pl.BlockSpec((None, hw, KV_A_WIDTH), lambda i, j: (i, jnp.maximum(j * per - 1, 0), 0))
    nxt = pl.BlockSpec((None, hw, KV_A_WIDTH),
                       lambda i, j: (i, jnp.minimum((j + 1) * per, last), 0))
    kern = functools.partial(_attn_a_kernel, layer=layer)
    return pl.pallas_call(
        kern,
        grid=(b, s // m),
        in_specs=[
            pl.BlockSpec(memory_space=pltpu.SMEM),
            cur(MIX_WIDTH),
            prev, cur(KV_A_WIDTH), nxt, prev, cur(KV_A_WIDTH), nxt,
            pl.BlockSpec((None, 1, MIX_WIDTH), lambda i, j: (layer, 0, 0)),
        ],
        out_specs=cur(MIX_WIDTH),
        out_shape=jax.ShapeDtypeStruct((b, s, MIX_WIDTH), BF16),
        scratch_shapes=[pltpu.VMEM((3, Q_ROWS + 2 * hw, 2 * Q_ROWS), F32)],
        compiler_params=_params(2),
        name="attn_a",
    )(sink, qa, ka, ka, ka, va, va, va, gain)


def _attn_b_kernel(q_ref, kp_ref, kc_ref, kn_ref, vp_ref, vc_ref, vn_ref,
                   o_ref, lse_ref, bias_ref, *, hw):
    m = q_ref.shape[0]
    n_sub = m // Q_ROWS
    n_pairs = MIX_WIDTH // LANES
    rows_per_head = LANES // N_HEADS

    @pl.when((pl.program_id(0) == 0) & (pl.program_id(1) == 0) & (pl.program_id(2) == 0))
    def _():
        _fill_band_bias(bias_ref, hw)

    def group(sb, p):
        cols = slice(p * LANES, (p + 1) * LANES)
        return lambda: (q_ref[sb * Q_ROWS:(sb + 1) * Q_ROWS, cols],
                        _key_window(kp_ref, kc_ref, kn_ref, sb, hw, cols),
                        _key_window(vp_ref, vc_ref, vn_ref, sb, hw, cols),
                        bias_ref[_band_case(sb, n_sub, 2)], None)

    lse_rows = []

    def emit(g, out_t, lse):
        sb, p = divmod(g, n_pairs)
        rows = slice(sb * Q_ROWS, (sb + 1) * Q_ROWS)
        o_ref[rows, p * LANES:(p + 1) * LANES] = out_t.T
        lse_rows.extend([lse[:, 0:Q_ROWS], lse[:, Q_ROWS:2 * Q_ROWS]])
        if len(lse_rows) == N_HEADS:
            lse_t = jnp.concatenate(
                [jnp.broadcast_to(r, (rows_per_head, Q_ROWS)) for r in lse_rows], axis=0)
            lse_ref[rows, :] = lse_t.T
            lse_rows.clear()

    _attend_groups([group(sb, p) for sb in range(n_sub) for p in range(n_pairs)], emit)


def _attention_b_branch(q, k, v, window):
    b, dilation, seq, w = q.shape
    hw = window // (2 * dilation)
    m = min(seq, 512)
    per = m // hw
    last = seq // hw - 1
    cur = lambda width: pl.BlockSpec((None, None, m, width), lambda i, r, j: (i, r, j, 0))
    prev = pl.BlockSpec((None, None, hw, w),
                        lambda i, r, j: (i, r, jnp.maximum(j * per - 1, 0), 0))
    nxt = pl.BlockSpec((None, None, hw, w),
                       lambda i, r, j: (i, r, jnp.minimum((j + 1) * per, last), 0))
    kern = functools.partial(_attn_b_kernel, hw=hw)
    return pl.pallas_call(
        kern,
        grid=(b, dilation, seq // m),
        in_specs=[cur(w), prev, cur(w), nxt, prev, cur(w), nxt],
        out_specs=[cur(w), cur(LANES)],
        out_shape=[jax.ShapeDtypeStruct((b, dilation, seq, w), F32),
                   jax.ShapeDtypeStruct((b, dilation, seq, LANES), F32)],
        scratch_shapes=[pltpu.VMEM((3, Q_ROWS + 2 * hw, 2 * Q_ROWS), F32)],
        compiler_params=_params(3),
        name=f"attn_b_d{dilation}",
    )(q, k, k, k, v, v, v)


def _out_kernel(x_ref, mod_ref, ya_ref, o1_ref, o2_ref, o3_ref, l1_ref, l2_ref, l3_ref,
                gb_ref, w_ref, o_ref, relay_o, relay_l):
    x = x_ref[0]
    tm = x.shape[0]
    n_groups = MIX_WIDTH // LANES

    def natural_order(src_ref, dil, dst, n_lane_groups):
        n = tm // dil
        for r in range(dil):
            for g in range(n_lane_groups):
                dst[g, pl.ds(r, n, stride=dil), :] = src_ref[r, :, g * LANES:(g + 1) * LANES]

    lses = []
    for i, (l_ref, dil) in enumerate(zip((l1_ref, l2_ref, l3_ref), DILATIONS)):
        if dil == 1:
            lses.append(l_ref[0])
        else:
            natural_order(l_ref, dil, relay_l.at[i], 1)
            lses.append(relay_l[i, 0])
    mx = jnp.maximum(jnp.maximum(lses[0], lses[1]), lses[2])
    es = [jnp.exp(l - mx) for l in lses]
    den = es[0] + es[1] + es[2]
    wts = [e / den for e in es]
    is_lo = lax.broadcasted_iota(jnp.int32, (tm, LANES), 1) < HEAD_DIM
    per_head = LANES // N_HEADS
    groups = [None] * n_groups
    for wt, o_r, dil in zip(wts, (o1_ref, o2_ref, o3_ref), DILATIONS):
        if dil != 1:
            natural_order(o_r, dil, relay_o, n_groups)
        for p in range(n_groups):
            vals = o_r[0, :, p * LANES:(p + 1) * LANES] if dil == 1 else relay_o[p]
            lo = wt[:, 2 * p * per_head:2 * p * per_head + 1]
            hi = wt[:, (2 * p + 1) * per_head:(2 * p + 1) * per_head + 1]
            term = jnp.where(is_lo, lo, hi) * vals
            groups[p] = term if groups[p] is None else groups[p] + term
    ssq = sum(jnp.sum(g * g, axis=1, keepdims=True) for g in groups)
    inv = lax.rsqrt(ssq / MIX_WIDTH + EPS)
    yb = jnp.concatenate(
        [((g * inv) * gb_ref[:, p * LANES:(p + 1) * LANES]).astype(BF16)
         for p, g in enumerate(groups)], axis=1)
    mix = jnp.dot(ya_ref[0], w_ref[0:MIX_WIDTH, :], preferred_element_type=F32)
    mix = mix + jnp.dot(yb, w_ref[MIX_WIDTH:2 * MIX_WIDTH, :], preferred_element_type=F32)
    o_ref[0] = x + mod_ref[0, 0, 5:6, :] * mix


def _merge_project(x, mod, layer, ya, outs, lses, gain_b, w_out, tm=512):
    b, s, d = x.shape
    tok = lambda w: pl.BlockSpec((1, tm, w), lambda i, j: (i, j, 0))
    res = lambda dil, w: pl.BlockSpec((None, dil, tm // dil, w), lambda i, j: (i, 0, j, 0))
    return pl.pallas_call(
        _out_kernel,
        grid=(b, s // tm),
        in_specs=[
            tok(d),
            pl.BlockSpec((1, 1, N_MOD, d), lambda i, j: (layer, i, 0, 0)),
            tok(MIX_WIDTH),
            *[res(dil, MIX_WIDTH) for dil in DILATIONS],
            *[res(dil, LANES) for dil in DILATIONS],
            pl.BlockSpec((None, 1, MIX_WIDTH), lambda i, j: (layer, 0, 0)),
            _resident((None, 2 * MIX_WIDTH, d), lambda i, j: (layer, 0, 0)),
        ],
        out_specs=tok(d),
        out_shape=jax.ShapeDtypeStruct(x.shape, F32),
        scratch_shapes=[pltpu.VMEM((MIX_WIDTH // LANES, tm, LANES), F32),
                        pltpu.VMEM((len(DILATIONS), 1, tm, LANES), F32)],
        compiler_params=_params(2),
        name="merge_out_proj",
    )(x, mod, ya, *outs, *lses, gain_b, w_out)


def _pair_kv_heads(n_heads):
    half = n_heads // 2
    return [h for p in range(half) for h in (p, p + half)]


def kernel(x, c, positions, ada_w, ada_b, norm_ffn1, ffn1_wi, ffn1_wo, norm_mix, w_in, sink,
           onorm_a, onorm_b, w_out, norm_ffn2, ffn2_wi, ffn2_wo, final_norm):
    depth = ada_w.shape[0]
    order = _pair_kv_heads(N_HEADS)
    cols_a = np.concatenate([np.arange(h * HEAD_DIM, (h + 1) * HEAD_DIM) for h in order])
    perm_in = np.concatenate([cols_a, np.arange(MIX_WIDTH, D_IN)])
    perm_out = np.concatenate([cols_a, np.arange(MIX_WIDTH, 2 * MIX_WIDTH)])
    w_in_b = w_in[:, :, perm_in].astype(BF16)
    w_out_b = w_out[:, perm_out, :].astype(BF16)
    gain_a = onorm_a[:, cols_a].reshape(depth, 1, MIX_WIDTH)
    gain_b = onorm_b.reshape(depth, 1, MIX_WIDTH)
    sink_p = sink[:, np.array(order)]
    wi1, wo1 = ffn1_wi.astype(BF16), ffn1_wo.astype(BF16)
    wi2, wo2 = ffn2_wi.astype(BF16), ffn2_wo.astype(BF16)
    g_ffn1 = norm_ffn1.reshape(depth, 1, D_MODEL)
    g_mix = norm_mix.reshape(depth, 1, D_MODEL)
    g_ffn2 = norm_ffn2.reshape(depth, 1, D_MODEL)
    g_final = final_norm.reshape(1, D_MODEL)

    mod = _modulation(c, ada_w, ada_b)
    cos, sin = _rope_tables(positions)
    for l in range(depth):
        x = _ffn(x, mod, l, 0, g_ffn1, wi1, wo1, g_final, False)
        qa, ka, va, qb, kb, vb = _project(x, mod, l, g_mix, cos, sin, w_in_b)
        ya = _attention_a(qa, ka, va, sink_p, gain_a, l)
        outs, lses = zip(*[_attention_b_branch(qb[i], kb[i], vb[i], w)
                           for i, (w, _) in enumerate(B_BRANCHES)])
        x = _merge_project(x, mod, l, ya, outs, lses, gain_b, w_out_b)
        x = _ffn(x, mod, l, 6, g_ffn2, wi2, wo2, g_final, l == depth - 1)
    return x
```

```python
import functools

import numpy as np
import jax
import jax.numpy as jnp
from jax import lax
from jax.experimental import pallas as pl
from jax.experimental.pallas import tpu as pltpu

D_MODEL = 1024
HEAD_DIM = 64
N_HEADS = 8
MIX_WIDTH = N_HEADS * HEAD_DIM
KV_A_WIDTH = 2 * HEAD_DIM
A_HALF_WINDOW = 128
B_BRANCHES = ((128, 1), (512, 4), (2048, 16))
DILATIONS = tuple(d for _, d in B_BRANCHES)
ROPE_THETA = 500000.0
ROPE_DIM = 16
D_FF = 2816
N_MOD = 9
D_IN = 2304
EPS = 1e-6
NEG_INF = -1e30

LANES = 128
Q_ROWS = 128
STAGE_SKEW = 2
F_CHUNK = 256
VMEM_LIMIT = 56 * 1024 * 1024

F32 = jnp.float32
BF16 = jnp.bfloat16


def _params(n_axes):
    return pltpu.CompilerParams(
        dimension_semantics=("arbitrary",) * n_axes, vmem_limit_bytes=VMEM_LIMIT)


def _resident(block_shape, index_map):
    return pl.BlockSpec(block_shape, index_map, pipeline_mode=pl.Buffered(1))


def _norm_mod(x, gain, shift, scale):
    y = x * lax.rsqrt(jnp.mean(x * x, axis=-1, keepdims=True) + EPS)
    return (y * gain) * (1.0 + scale) + shift


def _mod_kernel(c_ref, w_ref, b_ref, o_ref):
    ca = jax.nn.silu(c_ref[...]).astype(BF16)
    o_ref[0] = jnp.dot(ca, w_ref[0].astype(BF16), preferred_element_type=F32) + b_ref[0]


def _modulation(c, ada_w, ada_b):
    depth, d, n = ada_w.shape
    b = c.shape[0]
    tn = n // 8
    out = pl.pallas_call(
        _mod_kernel,
        grid=(depth, n // tn),
        in_specs=[
            pl.BlockSpec((b, d), lambda l, j: (0, 0)),
            pl.BlockSpec((1, d, tn), lambda l, j: (l, 0, j)),
            pl.BlockSpec((1, 1, tn), lambda l, j: (l, 0, j)),
        ],
        out_specs=pl.BlockSpec((1, b, tn), lambda l, j: (l, 0, j)),
        out_shape=jax.ShapeDtypeStruct((depth, b, n), F32),
        compiler_params=_params(2),
        name="adaln_mod",
    )(c, ada_w, ada_b.reshape(depth, 1, n))
    return out.reshape(depth, b, N_MOD, d)


def _ffn_kernel(x_ref, mod_ref, g_ref, wi_ref, wo_ref, fg_ref, o_ref, act_ref, *,
                mod_row, final_norm):
    x = x_ref[0]
    shift = mod_ref[0, 0, mod_row:mod_row + 1, :]
    scale = mod_ref[0, 0, mod_row + 1:mod_row + 2, :]
    gate_res = mod_ref[0, 0, mod_row + 2:mod_row + 3, :]
    hb = _norm_mod(x, g_ref[...], shift, scale).astype(BF16)
    for j in range(D_FF // F_CHUNK):
        lo = j * F_CHUNK
        gate = jnp.dot(hb, wi_ref[:, lo:lo + F_CHUNK], preferred_element_type=F32)
        up = jnp.dot(hb, wi_ref[:, D_FF + lo:D_FF + lo + F_CHUNK], preferred_element_type=F32)
        act_ref[:, lo:lo + F_CHUNK] = (jax.nn.silu(gate) * up).astype(BF16)
    y = jnp.dot(act_ref[...], wo_ref[...], preferred_element_type=F32)
    out = x + (0.5 * gate_res) * y
    if final_norm:
        out = out * lax.rsqrt(jnp.mean(out * out, axis=-1, keepdims=True) + EPS) * fg_ref[...]
    o_ref[0] = out


def _ffn(x, mod, layer, mod_row, gain, wi, wo, final_gain, final_norm, tm=512):
    b, s, d = x.shape
    kern = functools.partial(_ffn_kernel, mod_row=mod_row, final_norm=final_norm)
    return pl.pallas_call(
        kern,
        grid=(b, s // tm),
        in_specs=[
            pl.BlockSpec((1, tm, d), lambda i, j: (i, j, 0)),
            pl.BlockSpec((1, 1, N_MOD, d), lambda i, j: (layer, i, 0, 0)),
            pl.BlockSpec((None, 1, d), lambda i, j: (layer, 0, 0)),
            _resident((None, d, 2 * D_FF), lambda i, j: (layer, 0, 0)),
            _resident((None, D_FF, d), lambda i, j: (layer, 0, 0)),
            pl.BlockSpec((1, d), lambda i, j: (0, 0)),
        ],
        out_specs=pl.BlockSpec((1, tm, d), lambda i, j: (i, j, 0)),
        out_shape=jax.ShapeDtypeStruct(x.shape, F32),
        scratch_shapes=[pltpu.VMEM((tm, D_FF), BF16)],
        compiler_params=_params(2),
        name="ffn",
    )(x, mod, gain, wi, wo, final_gain)


def _rope_cs_kernel(pos_ref, inv_ref, cos_ref, sin_ref):
    ang = pos_ref[0].astype(F32) * inv_ref[...]
    cos_ref[0] = jnp.cos(ang)
    sin_ref[0] = jnp.sin(ang)


def _rope_tables(positions):
    b, s = positions.shape
    nf = ROPE_DIM // 2
    rows = s * nf // LANES
    inv = (ROPE_THETA ** (-np.arange(0, ROPE_DIM, 2, dtype=np.float64) / ROPE_DIM)).astype(np.float32)
    inv_tile = jnp.asarray(np.tile(inv, LANES // nf).reshape(1, LANES))
    pos_rep = jnp.repeat(positions, nf, axis=1).reshape(b, rows, LANES)
    spec = pl.BlockSpec((1, rows, LANES), lambda i: (i, 0, 0))
    cos, sin = pl.pallas_call(
        _rope_cs_kernel,
        grid=(b,),
        in_specs=[spec, pl.BlockSpec((1, LANES), lambda i: (0, 0))],
        out_specs=[spec, spec],
        out_shape=[jax.ShapeDtypeStruct((b, rows, LANES), F32)] * 2,
        compiler_params=_params(1),
        name="rope_cos_sin",
    )(pos_rep, inv_tile)
    return cos.reshape(b, s, nf), sin.reshape(b, s, nf)


def _proj_kernel(x_ref, mod_ref, g_ref, cos_ref, sin_ref, w_ref, *refs):
    qa_ref, ka_ref, va_ref = refs[:3]
    n_dil = len(DILATIONS)
    b_refs = [refs[3 + t * n_dil:3 + (t + 1) * n_dil] for t in range(3)]
    relays = refs[3 + 3 * n_dil:]
    x = x_ref[0]
    tm = x.shape[0]
    hb = _norm_mod(x, g_ref[...], mod_ref[0, 0, 3:4, :], mod_ref[0, 0, 4:5, :]).astype(BF16)

    lane = lax.broadcasted_iota(jnp.int32, (tm, LANES), 1)
    in_head = lane & (HEAD_DIM - 1)
    half = ROPE_DIM // 2
    cos8 = cos_ref[0]
    sin8 = sin_ref[0]
    cos_t = jnp.ones((tm, LANES), F32)
    sin_t = jnp.zeros((tm, LANES), F32)
    for f in range(half):
        sel = (in_head < ROPE_DIM) & ((in_head & (half - 1)) == f)
        cos_t = jnp.where(sel, cos8[:, f:f + 1], cos_t)
        sin_t = jnp.where(sel, sin8[:, f:f + 1], sin_t)
    first_half = in_head < half
    sin_t = jnp.where(first_half, -sin_t, sin_t)

    def rope(t):
        partner = jnp.where(first_half, pltpu.roll(t, LANES - half, axis=1),
                            pltpu.roll(t, half, axis=1))
        return t * cos_t + partner * sin_t

    def lane_groups(col, width, rotary, mult):
        t = jnp.dot(hb, w_ref[:, col:col + width], preferred_element_type=F32)
        for g in range(width // LANES):
            tg = t[:, g * LANES:(g + 1) * LANES]
            if rotary:
                tg = rope(tg)
            if mult is not None:
                tg = tg * mult
            yield g, tg

    scale = HEAD_DIM ** -0.5
    col = 0
    for out_ref, width, rotary, mult in ((qa_ref, MIX_WIDTH, True, scale),
                                         (ka_ref, KV_A_WIDTH, True, None),
                                         (va_ref, KV_A_WIDTH, False, None)):
        for g, tg in lane_groups(col, width, rotary, mult):
            out_ref[0, :, g * LANES:(g + 1) * LANES] = tg.astype(BF16)
        col += width
    n_groups = MIX_WIDTH // LANES
    for t, (outs, rotary, mult) in enumerate(((b_refs[0], True, scale),
                                              (b_refs[1], True, None),
                                              (b_refs[2], False, None))):
        src, dst = relays[2 * t], relays[2 * t + 1]
        for g, tg in lane_groups(col, MIX_WIDTH, rotary, mult):
            src[g] = tg
            outs[0][0, 0, :, g * LANES:(g + 1) * LANES] = tg.astype(BF16)
        for level in range(1, n_dil):
            coarse, fine = DILATIONS[level - 1], DILATIONS[level]
            ratio, n_coarse, n_fine = fine // coarse, tm // coarse, tm // fine
            for rc in range(coarse):
                for q in range(ratio):
                    for g in range(n_groups):
                        rows = src[g, pl.ds(rc * n_coarse + q, n_fine, stride=ratio), :]
                        r = rc + coarse * q
                        if level + 1 < n_dil:
                            dst[g, r * n_fine:(r + 1) * n_fine, :] = rows
                        outs[level][0, r, :, g * LANES:(g + 1) * LANES] = rows.astype(BF16)
            src, dst = dst, src
        col += MIX_WIDTH


def _project(x, mod, layer, gain, cos, sin, w_in, tm=512):
    b, s, d = x.shape
    nf = ROPE_DIM // 2
    tok = lambda w: pl.BlockSpec((1, tm, w), lambda i, j: (i, j, 0))
    res = lambda dil: pl.BlockSpec((1, dil, tm // dil, MIX_WIDTH), lambda i, j: (i, 0, j, 0))
    a_widths = (MIX_WIDTH, KV_A_WIDTH, KV_A_WIDTH)
    out_specs = [tok(w) for w in a_widths] + [res(dil) for _ in range(3) for dil in DILATIONS]
    out_shape = ([jax.ShapeDtypeStruct((b, s, w), BF16) for w in a_widths]
                 + [jax.ShapeDtypeStruct((b, dil, s // dil, MIX_WIDTH), BF16)
                    for _ in range(3) for dil in DILATIONS])
    outs = pl.pallas_call(
        _proj_kernel,
        grid=(b, s // tm),
        in_specs=[
            tok(d),
            pl.BlockSpec((1, 1, N_MOD, d), lambda i, j: (layer, i, 0, 0)),
            pl.BlockSpec((None, 1, d), lambda i, j: (layer, 0, 0)),
            tok(nf), tok(nf),
            _resident((None, d, D_IN), lambda i, j: (layer, 0, 0)),
        ],
        out_specs=out_specs,
        out_shape=out_shape,
        scratch_shapes=[pltpu.VMEM((MIX_WIDTH // LANES, tm, LANES), F32)] * 6,
        compiler_params=_params(2),
        name="in_proj_rope",
    )(x, mod, gain, cos, sin, w_in)
    n_dil = len(DILATIONS)
    qa, ka, va = outs[:3]
    qb, kb, vb = (outs[3 + t * n_dil:3 + (t + 1) * n_dil] for t in range(3))
    return qa, ka, va, qb, kb, vb


def _fill_band_bias(bias_ref, hw):
    n_keys = bias_ref.shape[1]
    jj = lax.broadcasted_iota(jnp.int32, (n_keys, 2 * Q_ROWS), 0)
    ii = lax.broadcasted_iota(jnp.int32, (n_keys, 2 * Q_ROWS), 1) & (Q_ROWS - 1)
    rel = jj - ii
    band = (rel >= 0) & (rel <= 2 * hw)
    for c, valid in enumerate((band & (jj >= hw), band, band & (jj < n_keys - hw))):
        bias_ref[c] = jnp.where(valid, 0.0, NEG_INF)


def _key_window(prev_ref, cur_ref, next_ref, sb, hw, cols):
    m = cur_ref.shape[0]
    lo, hi = sb * Q_ROWS - hw, (sb + 1) * Q_ROWS + hw
    parts = [cur_ref[max(lo, 0):min(hi, m), cols]]
    if lo < 0:
        parts.insert(0, prev_ref[:, cols])
    if hi > m:
        parts.append(next_ref[:, cols])
    return parts[0] if len(parts) == 1 else jnp.concatenate(parts, axis=0)


def _band_case(sb, n_sub, step_axis):
    if sb == 0:
        return jnp.where(pl.program_id(step_axis) == 0, 0, 1)
    if sb == n_sub - 1:
        return jnp.where(pl.program_id(step_axis) == pl.num_programs(step_axis) - 1, 2, 1)
    return 1


def _attend_groups(groups, emit):
    n = len(groups)
    state = [None] * n

    def scores(g):
        q_pair, k_win, v_win, bias, sink_row = groups[g]()
        is_lo = lax.broadcasted_iota(jnp.int32, q_pair.shape, 1) < HEAD_DIM
        zero = jnp.zeros_like(q_pair)
        q_both = jnp.concatenate(
            [jnp.where(is_lo, q_pair, zero), jnp.where(is_lo, zero, q_pair)], axis=0)
        s = lax.dot_general(k_win, q_both, (((1,), (1,)), ((), ())), preferred_element_type=F32)
        state[g] = (s + bias, v_win, sink_row)

    def softmax(g):
        s, v_win, sink_row = state[g]
        m = jnp.max(s, axis=0, keepdims=True)
        if sink_row is not None:
            m = jnp.maximum(m, sink_row)
        e = jnp.exp(s - m)
        l_keys = jnp.sum(e, axis=0, keepdims=True)
        denom = l_keys if sink_row is None else l_keys + jnp.exp(sink_row - m)
        state[g] = (e.astype(BF16), v_win, denom, m + jnp.log(l_keys))

    def values(g):
        e, v_win, denom, lse = state[g]
        pv = lax.dot_general(v_win, e, (((0,), (0,)), ((), ())), preferred_element_type=F32)
        inv = 1.0 / denom
        out_t = jnp.concatenate(
            [pv[0:HEAD_DIM, 0:Q_ROWS] * inv[:, 0:Q_ROWS],
             pv[HEAD_DIM:2 * HEAD_DIM, Q_ROWS:2 * Q_ROWS] * inv[:, Q_ROWS:]], axis=0)
        state[g] = None
        emit(g, out_t, lse)

    for t in range(n + 2 * STAGE_SKEW):
        if t < n:
            scores(t)
        if STAGE_SKEW <= t < n + STAGE_SKEW:
            softmax(t - STAGE_SKEW)
        if t >= 2 * STAGE_SKEW:
            values(t - 2 * STAGE_SKEW)


def _attn_a_kernel(sink_ref, q_ref, kp_ref, kc_ref, kn_ref, vp_ref, vc_ref, vn_ref, g_ref,
                   y_ref, bias_ref, *, layer):
    hw = A_HALF_WINDOW
    m = q_ref.shape[0]
    n_sub = m // Q_ROWS
    n_pairs = MIX_WIDTH // LANES

    @pl.when((pl.program_id(0) == 0) & (pl.program_id(1) == 0))
    def _():
        _fill_band_bias(bias_ref, hw)

    is_lo_query = lax.broadcasted_iota(jnp.int32, (1, 2 * Q_ROWS), 1) < Q_ROWS
    all_cols = slice(0, KV_A_WIDTH)

    def group(sb, p):
        def load():
            sink_row = jnp.where(is_lo_query, sink_ref[layer, p], sink_ref[layer, p + n_pairs])
            return (q_ref[sb * Q_ROWS:(sb + 1) * Q_ROWS, p * LANES:(p + 1) * LANES],
                    _key_window(kp_ref, kc_ref, kn_ref, sb, hw, all_cols),
                    _key_window(vp_ref, vc_ref, vn_ref, sb, hw, all_cols),
                    bias_ref[_band_case(sb, n_sub, 1)], sink_row)
        return load

    pending = []

    def emit(g, out_t, _):
        pending.append(out_t)
        if len(pending) < n_pairs:
            return
        sb = g // n_pairs
        ssq = sum(jnp.sum(o * o, axis=0, keepdims=True) for o in pending)
        inv = lax.rsqrt(ssq / MIX_WIDTH + EPS)
        for p, o in enumerate(pending):
            y = (o * inv).T * g_ref[:, p * LANES:(p + 1) * LANES]
            y_ref[sb * Q_ROWS:(sb + 1) * Q_ROWS, p * LANES:(p + 1) * LANES] = y.astype(BF16)
        pending.clear()

    _attend_groups([group(sb, p) for sb in range(n_sub) for p in range(n_pairs)], emit)


def _attention_a(qa, ka, va, sink, gain, layer, m=512):
    b, s, _ = qa.shape
    hw = A_HALF_WINDOW
    per = m // hw
    last = s // hw - 1
    cur = lambda w: pl.BlockSpec((None, m, w), lambda i, j: (i, j, 0))
    prev = pl.BlockSpec((None, hw, KV_A_WIDTH), lambda i, j: (i, jnp.maximum(j * per - 1, 0), 0))
    nxt = pl.BlockSpec((None, hw, KV_A_WIDTH),
                       lambda i, j: (i, jnp.minimum((j + 1) * per, last), 0))
    kern = functools.partial(_attn_a_kernel, layer=layer)
    return pl.pallas_call(
        kern,
        grid=(b, s // m),
        in_specs=[
            pl.BlockSpec(memory_space=pltpu.SMEM),
            cur(MIX_WIDTH),
            prev, cur(KV_A_WIDTH), nxt, prev, cur(KV_A_WIDTH), nxt,
            pl.BlockSpec((None, 1, MIX_WIDTH), lambda i, j: (layer, 0, 0)),
        ],
        out_specs=cur(MIX_WIDTH),
        out_shape=jax.ShapeDtypeStruct((b, s, MIX_WIDTH), BF16),
        scratch_shapes=[pltpu.VMEM((3, Q_ROWS + 2 * hw, 2 * Q_ROWS), F32)],
        compiler_params=_params(2),
        name="attn_a",
    )(sink, qa, ka, ka, ka, va, va, va, gain)


def _attn_b_kernel(q_ref, kp_ref, kc_ref, kn_ref, vp_ref, vc_ref, vn_ref,
                   o_ref, lse_ref, bias_ref, *, hw):
    m = q_ref.shape[0]
    n_sub = m // Q_ROWS
    n_pairs = MIX_WIDTH // LANES
    rows_per_head = LANES // N_HEADS

    @pl.when((pl.program_id(0) == 0) & (pl.program_id(1) == 0) & (pl.program_id(2) == 0))
    def _():
        _fill_band_bias(bias_ref, hw)

    def group(sb, p):
        cols = slice(p * LANES, (p + 1) * LANES)
        return lambda: (q_ref[sb * Q_ROWS:(sb + 1) * Q_ROWS, cols],
                        _key_window(kp_ref, kc_ref, kn_ref, sb, hw, cols),
                        _key_window(vp_ref, vc_ref, vn_ref, sb, hw, cols),
                        bias_ref[_band_case(sb, n_sub, 2)], None)

    lse_rows = []

    def emit(g, out_t, lse):
        sb, p = divmod(g, n_pairs)
        rows = slice(sb * Q_ROWS, (sb + 1) * Q_ROWS)
        o_ref[rows, p * LANES:(p + 1) * LANES] = out_t.T.astype(BF16)
        lse_rows.extend([lse[:, 0:Q_ROWS], lse[:, Q_ROWS:2 * Q_ROWS]])
        if len(lse_rows) == N_HEADS:
            lse_t = jnp.concatenate(
                [jnp.broadcast_to(r, (rows_per_head, Q_ROWS)) for r in lse_rows], axis=0)
            lse_ref[rows, :] = lse_t.T
            lse_rows.clear()

    _attend_groups([group(sb, p) for sb in range(n_sub) for p in range(n_pairs)], emit)


def _attention_b_branch(q, k, v, window):
    b, dilation, seq, w = q.shape
    hw = window // (2 * dilation)
    m = min(seq, 512)
    per = m // hw
    last = seq // hw - 1
    cur = lambda width: pl.BlockSpec((None, None, m, width), lambda i, r, j: (i, r, j, 0))
    prev = pl.BlockSpec((None, None, hw, w),
                        lambda i, r, j: (i, r, jnp.maximum(j * per - 1, 0), 0))
    nxt = pl.BlockSpec((None, None, hw, w),
                       lambda i, r, j: (i, r, jnp.minimum((j + 1) * per, last), 0))
    kern = functools.partial(_attn_b_kernel, hw=hw)
    return pl.pallas_call(
        kern,
        grid=(b, dilation, seq // m),
        in_specs=[cur(w), prev, cur(w), nxt, prev, cur(w), nxt],
        out_specs=[cur(w), cur(LANES)],
        out_shape=[jax.ShapeDtypeStruct((b, dilation, seq, w), BF16),
                   jax.ShapeDtypeStruct((b, dilation, seq, LANES), F32)],
        scratch_shapes=[pltpu.VMEM((3, Q_ROWS + 2 * hw, 2 * Q_ROWS), F32)],
        compiler_params=_params(3),
        name=f"attn_b_d{dilation}",
    )(q, k, k, k, v, v, v)


def _split_bf16(v):
    hi = v.astype(BF16)
    rest = v - hi.astype(F32)
    mid = rest.astype(BF16)
    return hi, mid, (rest - mid.astype(F32)).astype(BF16)


def _out_kernel(x_ref, mod_ref, ya_ref, o1_ref, o2_ref, o3_ref, l1_ref, l2_ref, l3_ref,
                gb_ref, w_ref, perm2_ref, perm3_ref, expand_ref, o_ref, relay_l, tmp_l):
    x = x_ref[0]
    tm = x.shape[0]

    def natural_order(src_ref, dil, dst, tmp, n_lane_groups):
        level = DILATIONS.index(dil)
        bufs = (tmp, dst)
        for lv in range(level, 0, -1):
            coarse, fine = DILATIONS[lv - 1], DILATIONS[lv]
            ratio, n_coarse, n_fine = fine // coarse, tm // coarse, tm // fine
            out = bufs[lv % 2]
            for rc in range(coarse):
                for q in range(ratio):
                    r = rc + coarse * q
                    for g in range(n_lane_groups):
                        if lv == level:
                            rows = src_ref[r, :, g * LANES:(g + 1) * LANES]
                        else:
                            rows = bufs[(lv + 1) % 2][g, r * n_fine:(r + 1) * n_fine, :]
                        out[g, pl.ds(rc * n_coarse + q, n_fine, stride=ratio), :] = rows

    lses = []
    for i, (l_ref, dil) in enumerate(zip((l1_ref, l2_ref, l3_ref), DILATIONS)):
        if dil == 1:
            lses.append(l_ref[0])
        else:
            natural_order(l_ref, dil, relay_l.at[i], tmp_l, 1)
            lses.append(relay_l[i, 0])
    mx = jnp.maximum(jnp.maximum(lses[0], lses[1]), lses[2])
    es = [jnp.exp(l - mx) for l in lses]
    den = es[0] + es[1] + es[2]

    def head_weight(e):
        terms = jnp.concatenate(_split_bf16(e / den), axis=1)
        return jnp.dot(terms, expand_ref[...], preferred_element_type=F32)

    o1 = o1_ref[0].astype(F32)
    o2 = jnp.dot(perm2_ref[...], o2_ref[...].reshape(tm, MIX_WIDTH), preferred_element_type=F32)
    o3 = jnp.dot(perm3_ref[...], o3_ref[...].reshape(tm, MIX_WIDTH), preferred_element_type=F32)
    ob = o1 + head_weight(es[1]) * (o2 - o1) + head_weight(es[2]) * (o3 - o1)
    inv = lax.rsqrt(jnp.mean(ob * ob, axis=1, keepdims=True) + EPS)
    yb = ((ob * inv) * gb_ref[...]).astype(BF16)
    mix = jnp.dot(ya_ref[0], w_ref[0:MIX_WIDTH, :], preferred_element_type=F32)
    mix = mix + jnp.dot(yb, w_ref[MIX_WIDTH:2 * MIX_WIDTH, :], preferred_element_type=F32)
    o_ref[0] = x + mod_ref[0, 0, 5:6, :] * mix


def _residue_permutation(tm, dil):
    token = np.arange(tm)
    perm = np.zeros((tm, tm), np.float32)
    perm[token, (token % dil) * (tm // dil) + token // dil] = 1.0
    return jnp.asarray(perm, BF16)


def _head_expansion(n_terms):
    expand = np.zeros((n_terms, LANES, MIX_WIDTH), np.float32)
    for h in range(N_HEADS):
        expand[:, h * (LANES // N_HEADS), h * HEAD_DIM:(h + 1) * HEAD_DIM] = 1.0
    return jnp.asarray(expand.reshape(n_terms * LANES, MIX_WIDTH), BF16)


def _merge_project(x, mod, layer, ya, outs, lses, gain_b, w_out, tm=512):
    b, s, d = x.shape
    tok = lambda w: pl.BlockSpec((1, tm, w), lambda i, j: (i, j, 0))
    res = lambda dil, w: pl.BlockSpec((None, dil, tm // dil, w), lambda i, j: (i, 0, j, 0))
    const = lambda shape: _resident(shape, lambda i, j: (0, 0))
    perms = [_residue_permutation(tm, dil) for dil in DILATIONS[1:]]
    expand = _head_expansion(3)
    return pl.pallas_call(
        _out_kernel,
        grid=(b, s // tm),
        in_specs=[
            tok(d),
            pl.BlockSpec((1, 1, N_MOD, d), lambda i, j: (layer, i, 0, 0)),
            tok(MIX_WIDTH),
            *[res(dil, MIX_WIDTH) for dil in DILATIONS],
            *[res(dil, LANES) for dil in DILATIONS],
            pl.BlockSpec((None, 1, MIX_WIDTH), lambda i, j: (layer, 0, 0)),
            _resident((None, 2 * MIX_WIDTH, d), lambda i, j: (layer, 0, 0)),
            const((tm, tm)), const((tm, tm)), const(expand.shape),
        ],
        out_specs=tok(d),
        out_shape=jax.ShapeDtypeStruct(x.shape, F32),
        scratch_shapes=[pltpu.VMEM((len(DILATIONS), 1, tm, LANES), F32),
                        pltpu.VMEM((1, tm, LANES), F32)],
        compiler_params=_params(2),
        name="merge_out_proj",
    )(x, mod, ya, *outs, *lses, gain_b, w_out, *perms, expand)


def _pair_kv_heads(t, axis):
    n_kv = KV_A_WIDTH // HEAD_DIM
    shape = t.shape
    split = shape[:axis] + (n_kv, N_HEADS // n_kv, HEAD_DIM) + shape[axis + 1:]
    return jnp.swapaxes(t.reshape(split), axis, axis + 1).reshape(shape)


def kernel(x, c, positions, ada_w, ada_b, norm_ffn1, ffn1_wi, ffn1_wo, norm_mix, w_in, sink,
           onorm_a, onorm_b, w_out, norm_ffn2, ffn2_wi, ffn2_wo, final_norm):
    depth = ada_w.shape[0]
    w_in_b = jnp.concatenate([_pair_kv_heads(w_in[:, :, :MIX_WIDTH].astype(BF16), 2),
                              w_in[:, :, MIX_WIDTH:].astype(BF16)], axis=2)
    w_out_b = jnp.concatenate([_pair_kv_heads(w_out[:, :MIX_WIDTH].astype(BF16), 1),
                               w_out[:, MIX_WIDTH:].astype(BF16)], axis=1)
    gain_a = _pair_kv_heads(onorm_a, 1).reshape(depth, 1, MIX_WIDTH)
    gain_b = onorm_b.reshape(depth, 1, MIX_WIDTH)
    wi1, wo1 = ffn1_wi.astype(BF16), ffn1_wo.astype(BF16)
    wi2, wo2 = ffn2_wi.astype(BF16), ffn2_wo.astype(BF16)
    g_ffn1 = norm_ffn1.reshape(depth, 1, D_MODEL)
    g_mix = norm_mix.reshape(depth, 1, D_MODEL)
    g_ffn2 = norm_ffn2.reshape(depth, 1, D_MODEL)
    g_final = final_norm.reshape(1, D_MODEL)

    mod = _modulation(c, ada_w, ada_b)
    cos, sin = _rope_tables(positions)
    for l in range(depth):
        x = _ffn(x, mod, l, 0, g_ffn1, wi1, wo1, g_final, False)
        qa, ka, va, qb, kb, vb = _project(x, mod, l, g_mix, cos, sin, w_in_b)
        ya = _attention_a(qa, ka, va, sink, gain_a, l)
        outs, lses = zip(*[_attention_b_branch(qb[i], kb[i], vb[i], w)
                           for i, (w, _) in enumerate(B_BRANCHES)])
        x = _merge_project(x, mod, l, ya, outs, lses, gain_b, w_out_b)
        x = _ffn(x, mod, l, 6, g_ffn2, wi2, wo2, g_final, l == depth - 1)
    return x
```

```python
import functools

import numpy as np
import jax
import jax.numpy as jnp
from jax import lax
from jax.experimental import pallas as pl
from jax.experimental.pallas import tpu as pltpu

D_MODEL = 1024
HEAD_DIM = 64
N_HEADS = 8
MIX_WIDTH = N_HEADS * HEAD_DIM
KV_A_WIDTH = 2 * HEAD_DIM
A_HALF_WINDOW = 128
B_BRANCHES = ((128, 1), (512, 4), (2048, 16))
DILATIONS = tuple(d for _, d in B_BRANCHES)
ROPE_THETA = 500000.0
ROPE_DIM = 16
D_FF = 2816
N_MOD = 9
D_IN = 2304
EPS = 1e-6
NEG_INF = -1e30

LANES = 128
Q_ROWS = 128
STAGE_SKEW = 2
F_CHUNK = 256
VMEM_LIMIT = 56 * 1024 * 1024

F32 = jnp.float32
BF16 = jnp.bfloat16


def _params(n_axes):
    return pltpu.CompilerParams(
        dimension_semantics=("arbitrary",) * n_axes, vmem_limit_bytes=VMEM_LIMIT)


def _resident(block_shape, index_map):
    return pl.BlockSpec(block_shape, index_map, pipeline_mode=pl.Buffered(1))


def _norm_mod(x, gain, shift, scale):
    y = x * lax.rsqrt(jnp.mean(x * x, axis=-1, keepdims=True) + EPS)
    return (y * gain) * (1.0 + scale) + shift


def _mod_kernel(c_ref, w_ref, b_ref, o_ref):
    ca = jax.nn.silu(c_ref[...]).astype(BF16)
    o_ref[0] = jnp.dot(ca, w_ref[0].astype(BF16), preferred_element_type=F32) + b_ref[0]


def _modulation(c, ada_w, ada_b):
    depth, d, n = ada_w.shape
    b = c.shape[0]
    tn = n // 8
    out = pl.pallas_call(
        _mod_kernel,
        grid=(depth, n // tn),
        in_specs=[
            pl.BlockSpec((b, d), lambda l, j: (0, 0)),
            pl.BlockSpec((1, d, tn), lambda l, j: (l, 0, j)),
            pl.BlockSpec((1, 1, tn), lambda l, j: (l, 0, j)),
        ],
        out_specs=pl.BlockSpec((1, b, tn), lambda l, j: (l, 0, j)),
        out_shape=jax.ShapeDtypeStruct((depth, b, n), F32),
        compiler_params=_params(2),
        name="adaln_mod",
    )(c, ada_w, ada_b.reshape(depth, 1, n))
    return out.reshape(depth, b, N_MOD, d)


def _ffn_kernel(x_ref, mod_ref, g_ref, wi_ref, wo_ref, fg_ref, o_ref, act_ref, *,
                mod_row, final_norm):
    x = x_ref[0]
    shift = mod_ref[0, 0, mod_row:mod_row + 1, :]
    scale = mod_ref[0, 0, mod_row + 1:mod_row + 2, :]
    gate_res = mod_ref[0, 0, mod_row + 2:mod_row + 3, :]
    hb = _norm_mod(x, g_ref[...], shift, scale).astype(BF16)
    for j in range(D_FF // F_CHUNK):
        lo = j * F_CHUNK
        gate = jnp.dot(hb, wi_ref[:, lo:lo + F_CHUNK], preferred_element_type=F32)
        up = jnp.dot(hb, wi_ref[:, D_FF + lo:D_FF + lo + F_CHUNK], preferred_element_type=F32)
        act_ref[:, lo:lo + F_CHUNK] = (jax.nn.silu(gate) * up).astype(BF16)
    y = jnp.dot(act_ref[...], wo_ref[...], preferred_element_type=F32)
    out = x + (0.5 * gate_res) * y
    if final_norm:
        out = out * lax.rsqrt(jnp.mean(out * out, axis=-1, keepdims=True) + EPS) * fg_ref[...]
    o_ref[0] = out


def _ffn(x, mod, layer, mod_row, gain, wi, wo, final_gain, final_norm, tm=512):
    b, s, d = x.shape
    kern = functools.partial(_ffn_kernel, mod_row=mod_row, final_norm=final_norm)
    return pl.pallas_call(
        kern,
        grid=(b, s // tm),
        in_specs=[
            pl.BlockSpec((1, tm, d), lambda i, j: (i, j, 0)),
            pl.BlockSpec((1, 1, N_MOD, d), lambda i, j: (layer, i, 0, 0)),
            pl.BlockSpec((None, 1, d), lambda i, j: (layer, 0, 0)),
            _resident((None, d, 2 * D_FF), lambda i, j: (layer, 0, 0)),
            _resident((None, D_FF, d), lambda i, j: (layer, 0, 0)),
            pl.BlockSpec((1, d), lambda i, j: (0, 0)),
        ],
        out_specs=pl.BlockSpec((1, tm, d), lambda i, j: (i, j, 0)),
        out_shape=jax.ShapeDtypeStruct(x.shape, F32),
        scratch_shapes=[pltpu.VMEM((tm, D_FF), BF16)],
        compiler_params=_params(2),
        name="ffn",
    )(x, mod, gain, wi, wo, final_gain)


_ROT_HALF = ROPE_DIM // 2


def _rotary_lane_order(t):
    g = t.reshape(t.shape[:-1] + (t.shape[-1] // LANES, LANES))
    a, b = _ROT_HALF, HEAD_DIM
    g = jnp.concatenate([g[..., :a], g[..., b:b + a], g[..., 2 * a:b], g[..., a:2 * a],
                         g[..., b + a:]], axis=-1)
    return g.reshape(t.shape)


def _first_head_lanes(shape):
    lane = lax.broadcasted_iota(jnp.int32, shape, len(shape) - 1)
    return (lane < _ROT_HALF) | ((lane >= ROPE_DIM) & (lane < HEAD_DIM + _ROT_HALF))


def _rope_table_kernel(pos_ref, inv_ref, cos_ref, sin_ref):
    ts = pos_ref.shape[-1]
    ang = inv_ref[...] * pos_ref[0].astype(F32)
    cos_d, sin_d = jnp.cos(ang), jnp.sin(ang)
    plain = HEAD_DIM - ROPE_DIM
    ones, zeros = jnp.ones((plain, ts), F32), jnp.zeros((plain, ts), F32)
    cos_t = jnp.concatenate([cos_d, cos_d, ones, cos_d, cos_d, ones], axis=0)
    sin_t = jnp.concatenate([-sin_d, -sin_d, zeros, sin_d, sin_d, zeros], axis=0)
    cos_ref[0] = cos_t.T
    sin_ref[0] = sin_t.T


def _rope_tables(positions, ts=512):
    b, s = positions.shape
    inv = (ROPE_THETA ** (-np.arange(0, ROPE_DIM, 2, dtype=np.float64) / ROPE_DIM)).astype(np.float32)
    out = pl.BlockSpec((1, ts, LANES), lambda i, j: (i, j, 0))
    return pl.pallas_call(
        _rope_table_kernel,
        grid=(b, s // ts),
        in_specs=[pl.BlockSpec((1, 1, ts), lambda i, j: (i, 0, j)),
                  pl.BlockSpec((_ROT_HALF, 1), lambda i, j: (0, 0))],
        out_specs=[out, out],
        out_shape=[jax.ShapeDtypeStruct((b, s, LANES), F32)] * 2,
        compiler_params=_params(2),
        name="rope_tables",
    )(positions.reshape(b, 1, s), jnp.asarray(inv.reshape(_ROT_HALF, 1)))


def _proj_kernel(x_ref, mod_ref, g_ref, cos_ref, sin_ref, w_ref, *refs):
    qa_ref, ka_ref, va_ref = refs[:3]
    n_dil = len(DILATIONS)
    b_refs = [refs[3 + t * n_dil:3 + (t + 1) * n_dil] for t in range(3)]
    relays = refs[3 + 3 * n_dil:]
    x = x_ref[0]
    tm = x.shape[0]
    hb = _norm_mod(x, g_ref[...], mod_ref[0, 0, 3:4, :], mod_ref[0, 0, 4:5, :]).astype(BF16)

    def rope(t):
        return t * cos_ref[0] + pltpu.roll(t, HEAD_DIM, axis=1) * sin_ref[0]

    proj = jnp.dot(hb, w_ref[...], preferred_element_type=F32)

    def lane_groups(col, width, rotary, mult):
        for g in range(width // LANES):
            tg = proj[:, col + g * LANES:col + (g + 1) * LANES]
            if rotary:
                tg = rope(tg)
            if mult is not None:
                tg = tg * mult
            yield g, tg

    scale = HEAD_DIM ** -0.5
    col = 0
    for out_ref, width, rotary, mult in ((qa_ref, MIX_WIDTH, True, scale),
                                         (ka_ref, KV_A_WIDTH, True, None),
                                         (va_ref, KV_A_WIDTH, False, None)):
        for g, tg in lane_groups(col, width, rotary, mult):
            out_ref[0, :, g * LANES:(g + 1) * LANES] = tg.astype(BF16)
        col += width
    n_groups = MIX_WIDTH // LANES
    for t, (outs, rotary, mult) in enumerate(((b_refs[0], True, scale),
                                              (b_refs[1], True, None),
                                              (b_refs[2], False, None))):
        src, dst = relays[2 * t], relays[2 * t + 1]
        for g, tg in lane_groups(col, MIX_WIDTH, rotary, mult):
            src[g] = tg
            outs[0][0, 0, :, g * LANES:(g + 1) * LANES] = tg.astype(BF16)
        for level in range(1, n_dil):
            coarse, fine = DILATIONS[level - 1], DILATIONS[level]
            ratio, n_coarse, n_fine = fine // coarse, tm // coarse, tm // fine
            for rc in range(coarse):
                for q in range(ratio):
                    for g in range(n_groups):
                        rows = src[g, pl.ds(rc * n_coarse + q, n_fine, stride=ratio), :]
                        r = rc + coarse * q
                        if level + 1 < n_dil:
                            dst[g, r * n_fine:(r + 1) * n_fine, :] = rows
                        outs[level][0, r, :, g * LANES:(g + 1) * LANES] = rows.astype(BF16)
            src, dst = dst, src
        col += MIX_WIDTH


def _project(x, mod, layer, gain, cos, sin, w_in, tm=512):
    b, s, d = x.shape
    tok = lambda w: pl.BlockSpec((1, tm, w), lambda i, j: (i, j, 0))
    res = lambda dil: pl.BlockSpec((1, dil, tm // dil, MIX_WIDTH), lambda i, j: (i, 0, j, 0))
    a_widths = (MIX_WIDTH, KV_A_WIDTH, KV_A_WIDTH)
    out_specs = [tok(w) for w in a_widths] + [res(dil) for _ in range(3) for dil in DILATIONS]
    out_shape = ([jax.ShapeDtypeStruct((b, s, w), BF16) for w in a_widths]
                 + [jax.ShapeDtypeStruct((b, dil, s // dil, MIX_WIDTH), BF16)
                    for _ in range(3) for dil in DILATIONS])
    outs = pl.pallas_call(
        _proj_kernel,
        grid=(b, s // tm),
        in_specs=[
            tok(d),
            pl.BlockSpec((1, 1, N_MOD, d), lambda i, j: (layer, i, 0, 0)),
            pl.BlockSpec((None, 1, d), lambda i, j: (layer, 0, 0)),
            tok(LANES), tok(LANES),
            _resident((None, d, D_IN), lambda i, j: (layer, 0, 0)),
        ],
        out_specs=out_specs,
        out_shape=out_shape,
        scratch_shapes=[pltpu.VMEM((MIX_WIDTH // LANES, tm, LANES), F32)] * 6,
        compiler_params=_params(2),
        name="in_proj_rope",
    )(x, mod, gain, cos, sin, w_in)
    n_dil = len(DILATIONS)
    qa, ka, va = outs[:3]
    qb, kb, vb = (outs[3 + t * n_dil:3 + (t + 1) * n_dil] for t in range(3))
    return qa, ka, va, qb, kb, vb


def _fill_band_bias(bias_ref, hw):
    n_keys = bias_ref.shape[1]
    jj = lax.broadcasted_iota(jnp.int32, (n_keys, 2 * Q_ROWS), 0)
    ii = lax.broadcasted_iota(jnp.int32, (n_keys, 2 * Q_ROWS), 1) & (Q_ROWS - 1)
    rel = jj - ii
    band = (rel >= 0) & (rel <= 2 * hw)
    for c, valid in enumerate((band & (jj >= hw), band, band & (jj < n_keys - hw))):
        bias_ref[c] = jnp.where(valid, 0.0, NEG_INF)


def _key_window(prev_ref, cur_ref, next_ref, sb, hw, cols):
    m = cur_ref.shape[0]
    lo, hi = sb * Q_ROWS - hw, (sb + 1) * Q_ROWS + hw
    parts = [cur_ref[max(lo, 0):min(hi, m), cols]]
    if lo < 0:
        parts.insert(0, prev_ref[:, cols])
    if hi > m:
        parts.append(next_ref[:, cols])
    return parts[0] if len(parts) == 1 else jnp.concatenate(parts, axis=0)


def _band_case(sb, n_sub, step_axis):
    if sb == 0:
        return jnp.where(pl.program_id(step_axis) == 0, 0, 1)
    if sb == n_sub - 1:
        return jnp.where(pl.program_id(step_axis) == pl.num_programs(step_axis) - 1, 2, 1)
    return 1


def _attend_groups(groups, emit):
    n = len(groups)
    state = [None] * n

    def scores(g):
        q_pair, k_win, v_win, bias, sink_row = groups[g]()
        is_lo = _first_head_lanes(q_pair.shape)
        zero = jnp.zeros_like(q_pair)
        q_both = jnp.concatenate(
            [jnp.where(is_lo, q_pair, zero), jnp.where(is_lo, zero, q_pair)], axis=0)
        s = lax.dot_general(k_win, q_both, (((1,), (1,)), ((), ())), preferred_element_type=F32)
        state[g] = (s + bias, v_win, sink_row)

    def softmax(g):
        s, v_win, sink_row = state[g]
        m = jnp.max(s, axis=0, keepdims=True)
        if sink_row is not None:
            m = jnp.maximum(m, sink_row)
        e = jnp.exp(s - m)
        l_keys = jnp.sum(e, axis=0, keepdims=True)
        denom = l_keys if sink_row is None else l_keys + jnp.exp(sink_row - m)
        state[g] = (e.astype(BF16), v_win, denom, m + jnp.log(l_keys))

    def values(g):
        e, v_win, denom, lse = state[g]
        pv = lax.dot_general(v_win, e, (((0,), (0,)), ((), ())), preferred_element_type=F32)
        inv = 1.0 / denom
        out_t = jnp.concatenate(
            [pv[0:HEAD_DIM, 0:Q_ROWS] * inv[:, 0:Q_ROWS],
             pv[HEAD_DIM:2 * HEAD_DIM, Q_ROWS:2 * Q_ROWS] * inv[:, Q_ROWS:]], axis=0)
        state[g] = None
        emit(g, out_t, lse)

    for t in range(n + 2 * STAGE_SKEW):
        if t < n:
            scores(t)
        if STAGE_SKEW <= t < n + STAGE_SKEW:
            softmax(t - STAGE_SKEW)
        if t >= 2 * STAGE_SKEW:
            values(t - 2 * STAGE_SKEW)


def _attn_a_kernel(sink_ref, q_ref, kp_ref, kc_ref, kn_ref, vp_ref, vc_ref, vn_ref, g_ref,
                   y_ref, bias_ref, *, layer):
    hw = A_HALF_WINDOW
    m = q_ref.shape[0]
    n_sub = m // Q_ROWS
    n_pairs = MIX_WIDTH // LANES

    @pl.when((pl.program_id(0) == 0) & (pl.program_id(1) == 0))
    def _():
        _fill_band_bias(bias_ref, hw)

    is_lo_query = lax.broadcasted_iota(jnp.int32, (1, 2 * Q_ROWS), 1) < Q_ROWS
    all_cols = slice(0, KV_A_WIDTH)

    def group(sb, p):
        def load():
            sink_row = jnp.where(is_lo_query, sink_ref[layer, p], sink_ref[layer, p + n_pairs])
            return (q_ref[sb * Q_ROWS:(sb + 1) * Q_ROWS, p * LANES:(p + 1) * LANES],
                    _key_window(kp_ref, kc_ref, kn_ref, sb, hw, all_cols),
                    _key_window(vp_ref, vc_ref, vn_ref, sb, hw, all_cols),
                    bias_ref[_band_case(sb, n_sub, 1)], sink_row)
        return load

    pending = []

    def emit(g, out_t, _):
        pending.append(out_t)
        if len(pending) < n_pairs:
            return
        sb = g // n_pairs
        ssq = sum(jnp.sum(o * o, axis=0, keepdims=True) for o in pending)
        inv = lax.rsqrt(ssq / MIX_WIDTH + EPS)
        for p, o in enumerate(pending):
            y = (o * inv).T * g_ref[:, p * LANES:(p + 1) * LANES]
            y_ref[sb * Q_ROWS:(sb + 1) * Q_ROWS, p * LANES:(p + 1) * LANES] = y.astype(BF16)
        pending.clear()

    _attend_groups([group(sb, p) for sb in range(n_sub) for p in range(n_pairs)], emit)


def _attention_a(qa, ka, va, sink, gain, layer, m=512):
    b, s, _ = qa.shape
    hw = A_HALF_WINDOW
    per = m // hw
    last = s // hw - 1
    cur = lambda w: pl.BlockSpec((None, m, w), lambda i, j: (i, j, 0))
    prev = pl.BlockSpec((None, hw, KV_A_WIDTH), lambda i, j: (i, jnp.maximum(j * per - 1, 0), 0))
    nxt = pl.BlockSpec((None, hw, KV_A_WIDTH),
                       lambda i, j: (i, jnp.minimum((j + 1) * per, last), 0))
    kern = functools.partial(_attn_a_kernel, layer=layer)
    return pl.pallas_call(
        kern,
        grid=(b, s // m),
        in_specs=[
            pl.BlockSpec(memory_space=pltpu.SMEM),
            cur(MIX_WIDTH),
            prev, cur(KV_A_WIDTH), nxt, prev, cur(KV_A_WIDTH), nxt,
            pl.BlockSpec((None, 1, MIX_WIDTH), lambda i, j: (layer, 0, 0)),
        ],
        out_specs=cur(MIX_WIDTH),
        out_shape=jax.ShapeDtypeStruct((b, s, MIX_WIDTH), BF16),
        scratch_shapes=[pltpu.VMEM((3, Q_ROWS + 2 * hw, 2 * Q_ROWS), F32)],
        compiler_params=_params(2),
        name="attn_a",
    )(sink, qa, ka, ka, ka, va, va, va, gain)


def _attn_b_kernel(q_ref, kp_ref, kc_ref, kn_ref, vp_ref, vc_ref, vn_ref,
                   o_ref, lse_ref, bias_ref, *, hw):
    m = q_ref.shape[0]
    n_sub = m // Q_ROWS
    n_pairs = MIX_WIDTH // LANES
    rows_per_head = LANES // N_HEADS

    @pl.when((pl.program_id(0) == 0) & (pl.program_id(1) == 0) & (pl.program_id(2) == 0))
    def _():
        _fill_band_bias(bias_ref, hw)

    def group(sb, p):
        cols = slice(p * LANES, (p + 1) * LANES)
        return lambda: (q_ref[sb * Q_ROWS:(sb + 1) * Q_ROWS, cols],
                        _key_window(kp_ref, kc_ref, kn_ref, sb, hw, cols),
                        _key_window(vp_ref, vc_ref, vn_ref, sb, hw, cols),
                        bias_ref[_band_case(sb, n_sub, 2)], None)

    lse_rows = []

    def emit(g, out_t, lse):
        sb, p = divmod(g, n_pairs)
        rows = slice(sb * Q_ROWS, (sb + 1) * Q_ROWS)
        o_ref[rows, p * LANES:(p + 1) * LANES] = out_t.T.astype(BF16)
        lse_rows.extend([lse[:, 0:Q_ROWS], lse[:, Q_ROWS:2 * Q_ROWS]])
        if len(lse_rows) == N_HEADS:
            lse_t = jnp.concatenate(
                [jnp.broadcast_to(r, (rows_per_head, Q_ROWS)) for r in lse_rows], axis=0)
            lse_ref[rows, :] = lse_t.T
            lse_rows.clear()

    _attend_groups([group(sb, p) for sb in range(n_sub) for p in range(n_pairs)], emit)


def _attention_b_branch(q, k, v, window):
    b, dilation, seq, w = q.shape
    hw = window // (2 * dilation)
    m = min(seq, 512)
    per = m // hw
    last = seq // hw - 1
    cur = lambda width: pl.BlockSpec((None, None, m, width), lambda i, r, j: (i, r, j, 0))
    prev = pl.BlockSpec((None, None, hw, w),
                        lambda i, r, j: (i, r, jnp.maximum(j * per - 1, 0), 0))
    nxt = pl.BlockSpec((None, None, hw, w),
                       lambda i, r, j: (i, r, jnp.minimum((j + 1) * per, last), 0))
    kern = functools.partial(_attn_b_kernel, hw=hw)
    return pl.pallas_call(
        kern,
        grid=(b, dilation, seq // m),
        in_specs=[cur(w), prev, cur(w), nxt, prev, cur(w), nxt],
        out_specs=[cur(w), cur(LANES)],
        out_shape=[jax.ShapeDtypeStruct((b, dilation, seq, w), BF16),
                   jax.ShapeDtypeStruct((b, dilation, seq, LANES), F32)],
        scratch_shapes=[pltpu.VMEM((3, Q_ROWS + 2 * hw, 2 * Q_ROWS), F32)],
        compiler_params=_params(3),
        name=f"attn_b_d{dilation}",
    )(q, k, k, k, v, v, v)


def _split_bf16(v):
    hi = v.astype(BF16)
    rest = v - hi.astype(F32)
    mid = rest.astype(BF16)
    return hi, mid, (rest - mid.astype(F32)).astype(BF16)


def _out_kernel(x_ref, mod_ref, ya_ref, o1_ref, o2_ref, o3_ref, l1_ref, l2_ref, l3_ref,
                gb_ref, w_ref, perm2_ref, perm3_ref, expand_ref, o_ref, relay_l, tmp_l):
    x = x_ref[0]
    tm = x.shape[0]

    def natural_order(src_ref, dil, dst, tmp, n_lane_groups):
        level = DILATIONS.index(dil)
        bufs = (tmp, dst)
        for lv in range(level, 0, -1):
            coarse, fine = DILATIONS[lv - 1], DILATIONS[lv]
            ratio, n_coarse, n_fine = fine // coarse, tm // coarse, tm // fine
            out = bufs[lv % 2]
            for rc in range(coarse):
                for q in range(ratio):
                    r = rc + coarse * q
                    for g in range(n_lane_groups):
                        if lv == level:
                            rows = src_ref[r, :, g * LANES:(g + 1) * LANES]
                        else:
                            rows = bufs[(lv + 1) % 2][g, r * n_fine:(r + 1) * n_fine, :]
                        out[g, pl.ds(rc * n_coarse + q, n_fine, stride=ratio), :] = rows

    lses = []
    for i, (l_ref, dil) in enumerate(zip((l1_ref, l2_ref, l3_ref), DILATIONS)):
        if dil == 1:
            lses.append(l_ref[0])
        else:
            natural_order(l_ref, dil, relay_l.at[i], tmp_l, 1)
            lses.append(relay_l[i, 0])
    mx = jnp.maximum(jnp.maximum(lses[0], lses[1]), lses[2])
    es = [jnp.exp(l - mx) for l in lses]
    den = es[0] + es[1] + es[2]

    def head_weight(e):
        terms = jnp.concatenate(_split_bf16(e / den), axis=1)
        return jnp.dot(terms, expand_ref[...], preferred_element_type=F32)

    o1 = o1_ref[0].astype(F32)
    o2 = jnp.dot(perm2_ref[...], o2_ref[...].reshape(tm, MIX_WIDTH), preferred_element_type=F32)
    o3 = jnp.dot(perm3_ref[...], o3_ref[...].reshape(tm, MIX_WIDTH), preferred_element_type=F32)
    ob = o1 + head_weight(es[1]) * (o2 - o1) + head_weight(es[2]) * (o3 - o1)
    inv = lax.rsqrt(jnp.mean(ob * ob, axis=1, keepdims=True) + EPS)
    yb = ((ob * inv) * gb_ref[...]).astype(BF16)
    mix = jnp.dot(ya_ref[0], w_ref[0:MIX_WIDTH, :], preferred_element_type=F32)
    mix = mix + jnp.dot(yb, w_ref[MIX_WIDTH:2 * MIX_WIDTH, :], preferred_element_type=F32)
    o_ref[0] = x + mod_ref[0, 0, 5:6, :] * mix


def _residue_permutation(tm, dil):
    token = np.arange(tm)
    perm = np.zeros((tm, tm), np.float32)
    perm[token, (token % dil) * (tm // dil) + token // dil] = 1.0
    return jnp.asarray(perm, BF16)


def _head_expansion(n_terms):
    expand = np.zeros((n_terms, LANES, MIX_WIDTH), np.float32)
    for h in range(N_HEADS):
        expand[:, h * (LANES // N_HEADS), h * HEAD_DIM:(h + 1) * HEAD_DIM] = 1.0
    return jnp.asarray(expand.reshape(n_terms * LANES, MIX_WIDTH), BF16)


def _merge_project(x, mod, layer, ya, outs, lses, gain_b, w_out, tm=512):
    b, s, d = x.shape
    tok = lambda w: pl.BlockSpec((1, tm, w), lambda i, j: (i, j, 0))
    res = lambda dil, w: pl.BlockSpec((None, dil, tm // dil, w), lambda i, j: (i, 0, j, 0))
    const = lambda shape: _resident(shape, lambda i, j: (0, 0))
    perms = [_residue_permutation(tm, dil) for dil in DILATIONS[1:]]
    expand = _head_expansion(3)
    return pl.pallas_call(
        _out_kernel,
        grid=(b, s // tm),
        in_specs=[
            tok(d),
            pl.BlockSpec((1, 1, N_MOD, d), lambda i, j: (layer, i, 0, 0)),
            tok(MIX_WIDTH),
            *[res(dil, MIX_WIDTH) for dil in DILATIONS],
            *[res(dil, LANES) for dil in DILATIONS],
            pl.BlockSpec((None, 1, MIX_WIDTH), lambda i, j: (layer, 0, 0)),
            _resident((None, 2 * MIX_WIDTH, d), lambda i, j: (layer, 0, 0)),
            const((tm, tm)), const((tm, tm)), const(expand.shape),
        ],
        out_specs=tok(d),
        out_shape=jax.ShapeDtypeStruct(x.shape, F32),
        scratch_shapes=[pltpu.VMEM((len(DILATIONS), 1, tm, LANES), F32),
                        pltpu.VMEM((1, tm, LANES), F32)],
        compiler_params=_params(2),
        name="merge_out_proj",
    )(x, mod, ya, *outs, *lses, gain_b, w_out, *perms, expand)


def _pair_kv_heads(t, axis):
    n_kv = KV_A_WIDTH // HEAD_DIM
    shape = t.shape
    split = shape[:axis] + (n_kv, N_HEADS // n_kv, HEAD_DIM) + shape[axis + 1:]
    return jnp.swapaxes(t.reshape(split), axis, axis + 1).reshape(shape)


def kernel(x, c, positions, ada_w, ada_b, norm_ffn1, ffn1_wi, ffn1_wo, norm_mix, w_in, sink,
           onorm_a, onorm_b, w_out, norm_ffn2, ffn2_wi, ffn2_wo, final_norm):
    depth = ada_w.shape[0]
    w_in16 = w_in.astype(BF16)
    a_kv, b_q, b_k = MIX_WIDTH + KV_A_WIDTH, MIX_WIDTH + 2 * KV_A_WIDTH, 2 * MIX_WIDTH + 2 * KV_A_WIDTH
    w_in_b = jnp.concatenate([
        _rotary_lane_order(_pair_kv_heads(w_in16[:, :, :MIX_WIDTH], 2)),
        _rotary_lane_order(w_in16[:, :, MIX_WIDTH:a_kv]),
        w_in16[:, :, a_kv:b_q],
        _rotary_lane_order(w_in16[:, :, b_q:b_k + MIX_WIDTH]),
        w_in16[:, :, b_k + MIX_WIDTH:]], axis=2)
    w_out_b = jnp.concatenate([_pair_kv_heads(w_out[:, :MIX_WIDTH].astype(BF16), 1),
                               w_out[:, MIX_WIDTH:].astype(BF16)], axis=1)
    gain_a = _pair_kv_heads(onorm_a, 1).reshape(depth, 1, MIX_WIDTH)
    gain_b = onorm_b.reshape(depth, 1, MIX_WIDTH)
    wi1, wo1 = ffn1_wi.astype(BF16), ffn1_wo.astype(BF16)
    wi2, wo2 = ffn2_wi.astype(BF16), ffn2_wo.astype(BF16)
    g_ffn1 = norm_ffn1.reshape(depth, 1, D_MODEL)
    g_mix = norm_mix.reshape(depth, 1, D_MODEL)
    g_ffn2 = norm_ffn2.reshape(depth, 1, D_MODEL)
    g_final = final_norm.reshape(1, D_MODEL)

    mod = _modulation(c, ada_w, ada_b)
    cos, sin = _rope_tables(positions)
    for l in range(depth):
        x = _ffn(x, mod, l, 0, g_ffn1, wi1, wo1, g_final, False)
        qa, ka, va, qb, kb, vb = _project(x, mod, l, g_mix, cos, sin, w_in_b)
        ya = _attention_a(qa, ka, va, sink, gain_a, l)
        outs, lses = zip(*[_attention_b_branch(qb[i], kb[i], vb[i], w)
                           for i, (w, _) in enumerate(B_BRANCHES)])
        x = _merge_project(x, mod, l, ya, outs, lses, gain_b, w_out_b)
        x = _ffn(x, mod, l, 6, g_ffn2, wi2, wo2, g_final, l == depth - 1)
    return x
```

```python
import functools

import numpy as np
import jax
import jax.numpy as jnp
from jax import lax
from jax.experimental import pallas as pl
from jax.experimental.pallas import tpu as pltpu

D_MODEL = 1024
HEAD_DIM = 64
N_HEADS = 8
MIX_WIDTH = N_HEADS * HEAD_DIM
KV_A_WIDTH = 2 * HEAD_DIM
A_HALF_WINDOW = 128
B_BRANCHES = ((128, 1), (512, 4), (2048, 16))
DILATIONS = tuple(d for _, d in B_BRANCHES)
ROPE_THETA = 500000.0
ROPE_DIM = 16
D_FF = 2816
N_MOD = 9
D_IN = 2304
EPS = 1e-6
NEG_INF = -1e30

LANES = 128
Q_ROWS = 128
STAGE_SKEW = 2
B_STEP_ROWS = 2048
F_CHUNK = 256
VMEM_LIMIT = 56 * 1024 * 1024

F32 = jnp.float32
BF16 = jnp.bfloat16


def _params(n_axes):
    return pltpu.CompilerParams(
        dimension_semantics=("arbitrary",) * n_axes, vmem_limit_bytes=VMEM_LIMIT)


def _resident(block_shape, index_map):
    return pl.BlockSpec(block_shape, index_map, pipeline_mode=pl.Buffered(1))


def _norm_mod(x, gain, shift, scale):
    y = x * lax.rsqrt(jnp.mean(x * x, axis=-1, keepdims=True) + EPS)
    return (y * gain) * (1.0 + scale) + shift


def _mod_kernel(c_ref, w_ref, b_ref, o_ref):
    ca = jax.nn.silu(c_ref[...]).astype(BF16)
    o_ref[0] = jnp.dot(ca, w_ref[0].astype(BF16), preferred_element_type=F32) + b_ref[0]


def _modulation(c, ada_w, ada_b):
    depth, d, n = ada_w.shape
    b = c.shape[0]
    tn = n // 8
    out = pl.pallas_call(
        _mod_kernel,
        grid=(depth, n // tn),
        in_specs=[
            pl.BlockSpec((b, d), lambda l, j: (0, 0)),
            pl.BlockSpec((1, d, tn), lambda l, j: (l, 0, j)),
            pl.BlockSpec((1, 1, tn), lambda l, j: (l, 0, j)),
        ],
        out_specs=pl.BlockSpec((1, b, tn), lambda l, j: (l, 0, j)),
        out_shape=jax.ShapeDtypeStruct((depth, b, n), F32),
        compiler_params=_params(2),
        name="adaln_mod",
    )(c, ada_w, ada_b.reshape(depth, 1, n))
    return out.reshape(depth, b, N_MOD, d)


def _ffn_kernel(x_ref, mod_ref, g_ref, wi_ref, wo_ref, fg_ref, o_ref, act_ref, *,
                mod_row, final_norm):
    x = x_ref[0]
    shift = mod_ref[0, 0, mod_row:mod_row + 1, :]
    scale = mod_ref[0, 0, mod_row + 1:mod_row + 2, :]
    gate_res = mod_ref[0, 0, mod_row + 2:mod_row + 3, :]
    hb = _norm_mod(x, g_ref[...], shift, scale).astype(BF16)
    for j in range(D_FF // F_CHUNK):
        lo = j * F_CHUNK
        gate = jnp.dot(hb, wi_ref[:, lo:lo + F_CHUNK], preferred_element_type=F32)
        up = jnp.dot(hb, wi_ref[:, D_FF + lo:D_FF + lo + F_CHUNK], preferred_element_type=F32)
        act_ref[:, lo:lo + F_CHUNK] = (jax.nn.silu(gate) * up).astype(BF16)
    y = jnp.dot(act_ref[...], wo_ref[...], preferred_element_type=F32)
    out = x + (0.5 * gate_res) * y
    if final_norm:
        out = out * lax.rsqrt(jnp.mean(out * out, axis=-1, keepdims=True) + EPS) * fg_ref[...]
    o_ref[0] = out


def _ffn(x, mod, layer, mod_row, gain, wi, wo, final_gain, final_norm, tm=512):
    b, s, d = x.shape
    kern = functools.partial(_ffn_kernel, mod_row=mod_row, final_norm=final_norm)
    return pl.pallas_call(
        kern,
        grid=(b, s // tm),
        in_specs=[
            pl.BlockSpec((1, tm, d), lambda i, j: (i, j, 0)),
            pl.BlockSpec((1, 1, N_MOD, d), lambda i, j: (layer, i, 0, 0)),
            pl.BlockSpec((None, 1, d), lambda i, j: (layer, 0, 0)),
            _resident((None, d, 2 * D_FF), lambda i, j: (layer, 0, 0)),
            _resident((None, D_FF, d), lambda i, j: (layer, 0, 0)),
            pl.BlockSpec((1, d), lambda i, j: (0, 0)),
        ],
        out_specs=pl.BlockSpec((1, tm, d), lambda i, j: (i, j, 0)),
        out_shape=jax.ShapeDtypeStruct(x.shape, F32),
        scratch_shapes=[pltpu.VMEM((tm, D_FF), BF16)],
        compiler_params=_params(2),
        name="ffn",
    )(x, mod, gain, wi, wo, final_gain)


_ROT_HALF = ROPE_DIM // 2


def _rotary_lane_order(t):
    g = t.reshape(t.shape[:-1] + (t.shape[-1] // LANES, LANES))
    a, b = _ROT_HALF, HEAD_DIM
    g = jnp.concatenate([g[..., :a], g[..., b:b + a], g[..., 2 * a:b], g[..., a:2 * a],
                         g[..., b + a:]], axis=-1)
    return g.reshape(t.shape)


def _first_head_lanes(shape):
    lane = lax.broadcasted_iota(jnp.int32, shape, len(shape) - 1)
    return (lane < _ROT_HALF) | ((lane >= ROPE_DIM) & (lane < HEAD_DIM + _ROT_HALF))


def _rope_table_kernel(pos_ref, inv_ref, cos_ref, sin_ref):
    ts = pos_ref.shape[-1]
    ang = inv_ref[...] * pos_ref[0].astype(F32)
    cos_d, sin_d = jnp.cos(ang), jnp.sin(ang)
    plain = HEAD_DIM - ROPE_DIM
    ones, zeros = jnp.ones((plain, ts), F32), jnp.zeros((plain, ts), F32)
    cos_t = jnp.concatenate([cos_d, cos_d, ones, cos_d, cos_d, ones], axis=0)
    sin_t = jnp.concatenate([-sin_d, -sin_d, zeros, sin_d, sin_d, zeros], axis=0)
    cos_ref[0] = cos_t.T
    sin_ref[0] = sin_t.T


def _rope_tables(positions, ts=512):
    b, s = positions.shape
    inv = (ROPE_THETA ** (-np.arange(0, ROPE_DIM, 2, dtype=np.float64) / ROPE_DIM)).astype(np.float32)
    out = pl.BlockSpec((1, ts, LANES), lambda i, j: (i, j, 0))
    return pl.pallas_call(
        _rope_table_kernel,
        grid=(b, s // ts),
        in_specs=[pl.BlockSpec((1, 1, ts), lambda i, j: (i, 0, j)),
                  pl.BlockSpec((_ROT_HALF, 1), lambda i, j: (0, 0))],
        out_specs=[out, out],
        out_shape=[jax.ShapeDtypeStruct((b, s, LANES), F32)] * 2,
        compiler_params=_params(2),
        name="rope_tables",
    )(positions.reshape(b, 1, s), jnp.asarray(inv.reshape(_ROT_HALF, 1)))


def _proj_kernel(x_ref, mod_ref, g_ref, cos_ref, sin_ref, w_ref, *refs):
    qa_ref, ka_ref, va_ref = refs[:3]
    n_dil = len(DILATIONS)
    b_refs = [refs[3 + t * n_dil:3 + (t + 1) * n_dil] for t in range(3)]
    relays = refs[3 + 3 * n_dil:]
    x = x_ref[0]
    tm = x.shape[0]
    hb = _norm_mod(x, g_ref[...], mod_ref[0, 0, 3:4, :], mod_ref[0, 0, 4:5, :]).astype(BF16)

    def rope(t):
        return t * cos_ref[0] + pltpu.roll(t, HEAD_DIM, axis=1) * sin_ref[0]

    proj = jnp.dot(hb, w_ref[...], preferred_element_type=F32)

    def lane_groups(col, width, rotary, mult):
        for g in range(width // LANES):
            tg = proj[:, col + g * LANES:col + (g + 1) * LANES]
            if rotary:
                tg = rope(tg)
            if mult is not None:
                tg = tg * mult
            yield g, tg

    scale = HEAD_DIM ** -0.5
    col = 0
    for out_ref, width, rotary, mult in ((qa_ref, MIX_WIDTH, True, scale),
                                         (ka_ref, KV_A_WIDTH, True, None),
                                         (va_ref, KV_A_WIDTH, False, None)):
        for g, tg in lane_groups(col, width, rotary, mult):
            out_ref[0, :, g * LANES:(g + 1) * LANES] = tg.astype(BF16)
        col += width
    n_groups = MIX_WIDTH // LANES
    for t, (outs, rotary, mult) in enumerate(((b_refs[0], True, scale),
                                              (b_refs[1], True, None),
                                              (b_refs[2], False, None))):
        src, dst = relays[2 * t], relays[2 * t + 1]
        for g, tg in lane_groups(col, MIX_WIDTH, rotary, mult):
            src[g] = tg
            outs[0][0, 0, :, g * LANES:(g + 1) * LANES] = tg.astype(BF16)
        for level in range(1, n_dil):
            coarse, fine = DILATIONS[level - 1], DILATIONS[level]
            ratio, n_coarse, n_fine = fine // coarse, tm // coarse, tm // fine
            for rc in range(coarse):
                for q in range(ratio):
                    for g in range(n_groups):
                        rows = src[g, pl.ds(rc * n_coarse + q, n_fine, stride=ratio), :]
                        r = rc + coarse * q
                        if level + 1 < n_dil:
                            dst[g, r * n_fine:(r + 1) * n_fine, :] = rows
                        outs[level][0, r, :, g * LANES:(g + 1) * LANES] = rows.astype(BF16)
            src, dst = dst, src
        col += MIX_WIDTH


def _project(x, mod, layer, gain, cos, sin, w_in, tm=512):
    b, s, d = x.shape
    tok = lambda w: pl.BlockSpec((1, tm, w), lambda i, j: (i, j, 0))
    res = lambda dil: pl.BlockSpec((1, dil, tm // dil, MIX_WIDTH), lambda i, j: (i, 0, j, 0))
    a_widths = (MIX_WIDTH, KV_A_WIDTH, KV_A_WIDTH)
    out_specs = [tok(w) for w in a_widths] + [res(dil) for _ in range(3) for dil in DILATIONS]
    out_shape = ([jax.ShapeDtypeStruct((b, s, w), BF16) for w in a_widths]
                 + [jax.ShapeDtypeStruct((b, dil, s // dil, MIX_WIDTH), BF16)
                    for _ in range(3) for dil in DILATIONS])
    outs = pl.pallas_call(
        _proj_kernel,
        grid=(b, s // tm),
        in_specs=[
            tok(d),
            pl.BlockSpec((1, 1, N_MOD, d), lambda i, j: (layer, i, 0, 0)),
            pl.BlockSpec((None, 1, d), lambda i, j: (layer, 0, 0)),
            tok(LANES), tok(LANES),
            _resident((None, d, D_IN), lambda i, j: (layer, 0, 0)),
        ],
        out_specs=out_specs,
        out_shape=out_shape,
        scratch_shapes=[pltpu.VMEM((MIX_WIDTH // LANES, tm, LANES), F32)] * 6,
        compiler_params=_params(2),
        name="in_proj_rope",
    )(x, mod, gain, cos, sin, w_in)
    n_dil = len(DILATIONS)
    qa, ka, va = outs[:3]
    qb, kb, vb = (outs[3 + t * n_dil:3 + (t + 1) * n_dil] for t in range(3))
    return qa, ka, va, qb, kb, vb


def _fill_band_bias(bias_ref, hw):
    n_keys = bias_ref.shape[1]
    jj = lax.broadcasted_iota(jnp.int32, (n_keys, 2 * Q_ROWS), 0)
    ii = lax.broadcasted_iota(jnp.int32, (n_keys, 2 * Q_ROWS), 1) & (Q_ROWS - 1)
    rel = jj - ii
    band = (rel >= 0) & (rel <= 2 * hw)
    for c, valid in enumerate((band & (jj >= hw), band, band & (jj < n_keys - hw))):
        bias_ref[c] = jnp.where(valid, 0.0, NEG_INF)


def _key_window(prev_ref, cur_ref, next_ref, sb, hw, cols):
    m = cur_ref.shape[0]
    lo, hi = sb * Q_ROWS - hw, (sb + 1) * Q_ROWS + hw
    parts = [cur_ref[max(lo, 0):min(hi, m), cols]]
    if lo < 0:
        parts.insert(0, prev_ref[:, cols])
    if hi > m:
        parts.append(next_ref[:, cols])
    return parts[0] if len(parts) == 1 else jnp.concatenate(parts, axis=0)


def _band_case(sb, n_sub, step_axis):
    if sb == 0:
        return jnp.where(pl.program_id(step_axis) == 0, 0, 1)
    if sb == n_sub - 1:
        return jnp.where(pl.program_id(step_axis) == pl.num_programs(step_axis) - 1, 2, 1)
    return 1


def _attend_groups(groups, emit):
    n = len(groups)
    state = [None] * n

    def scores(g):
        q_pair, k_win, v_win, bias, sink_row = groups[g]()
        is_lo = _first_head_lanes(q_pair.shape)
        zero = jnp.zeros_like(q_pair)
        q_both = jnp.concatenate(
            [jnp.where(is_lo, q_pair, zero), jnp.where(is_lo, zero, q_pair)], axis=0)
        s = lax.dot_general(k_win, q_both, (((1,), (1,)), ((), ())), preferred_element_type=F32)
        state[g] = (s + bias, v_win, sink_row)

    def softmax(g):
        s, v_win, sink_row = state[g]
        m = jnp.max(s, axis=0, keepdims=True)
        if sink_row is not None:
            m = jnp.maximum(m, sink_row)
        e = jnp.exp(s - m)
        l_keys = jnp.sum(e, axis=0, keepdims=True)
        denom = l_keys if sink_row is None else l_keys + jnp.exp(sink_row - m)
        state[g] = (e.astype(BF16), v_win, denom, m + jnp.log(l_keys))

    def values(g):
        e, v_win, denom, lse = state[g]
        pv = lax.dot_general(v_win, e, (((0,), (0,)), ((), ())), preferred_element_type=F32)
        inv = 1.0 / denom
        out_t = jnp.concatenate(
            [pv[0:HEAD_DIM, 0:Q_ROWS] * inv[:, 0:Q_ROWS],
             pv[HEAD_DIM:2 * HEAD_DIM, Q_ROWS:2 * Q_ROWS] * inv[:, Q_ROWS:]], axis=0)
        state[g] = None
        emit(g, out_t, lse)

    for t in range(n + 2 * STAGE_SKEW):
        if t < n:
            scores(t)
        if STAGE_SKEW <= t < n + STAGE_SKEW:
            softmax(t - STAGE_SKEW)
        if t >= 2 * STAGE_SKEW:
            values(t - 2 * STAGE_SKEW)


def _attn_a_kernel(sink_ref, q_ref, kp_ref, kc_ref, kn_ref, vp_ref, vc_ref, vn_ref, g_ref,
                   y_ref, bias_ref, *, layer):
    hw = A_HALF_WINDOW
    m = q_ref.shape[0]
    n_sub = m // Q_ROWS
    n_pairs = MIX_WIDTH // LANES

    @pl.when((pl.program_id(0) == 0) & (pl.program_id(1) == 0))
    def _():
        _fill_band_bias(bias_ref, hw)

    is_lo_query = lax.broadcasted_iota(jnp.int32, (1, 2 * Q_ROWS), 1) < Q_ROWS
    all_cols = slice(0, KV_A_WIDTH)

    def group(sb, p):
        def load():
            sink_row = jnp.where(is_lo_query, sink_ref[layer, p], sink_ref[layer, p + n_pairs])
            return (q_ref[sb * Q_ROWS:(sb + 1) * Q_ROWS, p * LANES:(p + 1) * LANES],
                    _key_window(kp_ref, kc_ref, kn_ref, sb, hw, all_cols),
                    _key_window(vp_ref, vc_ref, vn_ref, sb, hw, all_cols),
                    bias_ref[_band_case(sb, n_sub, 1)], sink_row)
        return load

    pending = []

    def emit(g, out_t, _):
        pending.append(out_t)
        if len(pending) < n_pairs:
            return
        sb = g // n_pairs
        ssq = sum(jnp.sum(o * o, axis=0, keepdims=True) for o in pending)
        inv = lax.rsqrt(ssq / MIX_WIDTH + EPS)
        for p, o in enumerate(pending):
            y = (o * inv).T * g_ref[:, p * LANES:(p + 1) * LANES]
            y_ref[sb * Q_ROWS:(sb + 1) * Q_ROWS, p * LANES:(p + 1) * LANES] = y.astype(BF16)
        pending.clear()

    _attend_groups([group(sb, p) for sb in range(n_sub) for p in range(n_pairs)], emit)


def _attention_a(qa, ka, va, sink, gain, layer, m=2048):
    b, s, _ = qa.shape
    hw = A_HALF_WINDOW
    per = m // hw
    last = s // hw - 1
    cur = lambda w: pl.BlockSpec((None, m, w), lambda i, j: (i, j, 0))
    prev = pl.BlockSpec((None, hw, KV_A_WIDTH), lambda i, j: (i, jnp.maximum(j * per - 1, 0), 0))
    nxt = pl.BlockSpec((None, hw, KV_A_WIDTH),
                       lambda i, j: (i, jnp.minimum((j + 1) * per, last), 0))
    kern = functools.partial(_attn_a_kernel, layer=layer)
    return pl.pallas_call(
        kern,
        grid=(b, s // m),
        in_specs=[
            pl.BlockSpec(memory_space=pltpu.SMEM),
            cur(MIX_WIDTH),
            prev, cur(KV_A_WIDTH), nxt, prev, cur(KV_A_WIDTH), nxt,
            pl.BlockSpec((None, 1, MIX_WIDTH), lambda i, j: (layer, 0, 0)),
        ],
        out_specs=cur(MIX_WIDTH),
        out_shape=jax.ShapeDtypeStruct((b, s, MIX_WIDTH), BF16),
        scratch_shapes=[pltpu.VMEM((3, Q_ROWS + 2 * hw, 2 * Q_ROWS), F32)],
        compiler_params=_params(2),
        name="attn_a",
    )(sink, qa, ka, ka, ka, va, va, va, gain)


def _attn_b_kernel(q_ref, kp_ref, kc_ref, kn_ref, vp_ref, vc_ref, vn_ref,
                   o_ref, lse_ref, bias_ref, *, hw):
    n_res, m = q_ref.shape[0], q_ref.shape[1]
    n_sub = m // Q_ROWS
    n_pairs = MIX_WIDTH // LANES
    rows_per_head = LANES // N_HEADS

    @pl.when((pl.program_id(0) == 0) & (pl.program_id(1) == 0) & (pl.program_id(2) == 0))
    def _():
        _fill_band_bias(bias_ref, hw)

    def group(r, sb, p):
        cols = slice(p * LANES, (p + 1) * LANES)
        return lambda: (q_ref[r, sb * Q_ROWS:(sb + 1) * Q_ROWS, cols],
                        _key_window(kp_ref.at[r], kc_ref.at[r], kn_ref.at[r], sb, hw, cols),
                        _key_window(vp_ref.at[r], vc_ref.at[r], vn_ref.at[r], sb, hw, cols),
                        bias_ref[_band_case(sb, n_sub, 2)], None)

    lse_rows = []

    def emit(g, out_t, lse):
        r, sb, p = g // (n_sub * n_pairs), (g // n_pairs) % n_sub, g % n_pairs
        rows = slice(sb * Q_ROWS, (sb + 1) * Q_ROWS)
        o_ref[r, rows, p * LANES:(p + 1) * LANES] = out_t.T.astype(BF16)
        lse_rows.extend([lse[:, 0:Q_ROWS], lse[:, Q_ROWS:2 * Q_ROWS]])
        if len(lse_rows) == N_HEADS:
            lse_t = jnp.concatenate(
                [jnp.broadcast_to(v, (rows_per_head, Q_ROWS)) for v in lse_rows], axis=0)
            lse_ref[r, rows, :] = lse_t.T
            lse_rows.clear()

    _attend_groups([group(r, sb, p) for r in range(n_res) for sb in range(n_sub)
                    for p in range(n_pairs)], emit)


def _attention_b_branch(q, k, v, window):
    b, dilation, seq, w = q.shape
    hw = window // (2 * dilation)
    m = min(seq, B_STEP_ROWS)
    n_res = min(dilation, B_STEP_ROWS // m)
    per = m // hw
    last = seq // hw - 1
    cur = lambda width: pl.BlockSpec((None, n_res, m, width), lambda i, r, j: (i, r, j, 0))
    prev = pl.BlockSpec((None, n_res, hw, w),
                        lambda i, r, j: (i, r, jnp.maximum(j * per - 1, 0), 0))
    nxt = pl.BlockSpec((None, n_res, hw, w),
                       lambda i, r, j: (i, r, jnp.minimum((j + 1) * per, last), 0))
    kern = functools.partial(_attn_b_kernel, hw=hw)
    return pl.pallas_call(
        kern,
        grid=(b, dilation // n_res, seq // m),
        in_specs=[cur(w), prev, cur(w), nxt, prev, cur(w), nxt],
        out_specs=[cur(w), cur(LANES)],
        out_shape=[jax.ShapeDtypeStruct((b, dilation, seq, w), BF16),
                   jax.ShapeDtypeStruct((b, dilation, seq, LANES), F32)],
        scratch_shapes=[pltpu.VMEM((3, Q_ROWS + 2 * hw, 2 * Q_ROWS), F32)],
        compiler_params=_params(3),
        name=f"attn_b_d{dilation}",
    )(q, k, k, k, v, v, v)


def _split_bf16(v):
    hi = v.astype(BF16)
    rest = v - hi.astype(F32)
    mid = rest.astype(BF16)
    return hi, mid, (rest - mid.astype(F32)).astype(BF16)


def _out_kernel(x_ref, mod_ref, ya_ref, o1_ref, o2_ref, o3_ref, l1_ref, l2_ref, l3_ref,
                gb_ref, w_ref, perm2_ref, perm3_ref, expand_ref, o_ref, relay_l, tmp_l):
    x = x_ref[0]
    tm = x.shape[0]

    def natural_order(src_ref, dil, dst, tmp, n_lane_groups):
        level = DILATIONS.index(dil)
        bufs = (tmp, dst)
        for lv in range(level, 0, -1):
            coarse, fine = DILATIONS[lv - 1], DILATIONS[lv]
            ratio, n_coarse, n_fine = fine // coarse, tm // coarse, tm // fine
            out = bufs[lv % 2]
            for rc in range(coarse):
                for q in range(ratio):
                    r = rc + coarse * q
                    for g in range(n_lane_groups):
                        if lv == level:
                            rows = src_ref[r, :, g * LANES:(g + 1) * LANES]
                        else:
                            rows = bufs[(lv + 1) % 2][g, r * n_fine:(r + 1) * n_fine, :]
                        out[g, pl.ds(rc * n_coarse + q, n_fine, stride=ratio), :] = rows

    lses = []
    for i, (l_ref, dil) in enumerate(zip((l1_ref, l2_ref, l3_ref), DILATIONS)):
        if dil == 1:
            lses.append(l_ref[0])
        else:
            natural_order(l_ref, dil, relay_l.at[i], tmp_l, 1)
            lses.append(relay_l[i, 0])
    mx = jnp.maximum(jnp.maximum(lses[0], lses[1]), lses[2])
    es = [jnp.exp(l - mx) for l in lses]
    den = es[0] + es[1] + es[2]

    def head_weight(e):
        terms = jnp.concatenate(_split_bf16(e / den), axis=1)
        return jnp.dot(terms, expand_ref[...], preferred_element_type=F32)

    o1 = o1_ref[0].astype(F32)
    o2 = jnp.dot(perm2_ref[...], o2_ref[...].reshape(tm, MIX_WIDTH), preferred_element_type=F32)
    o3 = jnp.dot(perm3_ref[...], o3_ref[...].reshape(tm, MIX_WIDTH), preferred_element_type=F32)
    ob = o1 + head_weight(es[1]) * (o2 - o1) + head_weight(es[2]) * (o3 - o1)
    inv = lax.rsqrt(jnp.mean(ob * ob, axis=1, keepdims=True) + EPS)
    yb = ((ob * inv) * gb_ref[...]).astype(BF16)
    mix = jnp.dot(ya_ref[0], w_ref[0:MIX_WIDTH, :], preferred_element_type=F32)
    mix = mix + jnp.dot(yb, w_ref[MIX_WIDTH:2 * MIX_WIDTH, :], preferred_element_type=F32)
    o_ref[0] = x + mod_ref[0, 0, 5:6, :] * mix


def _residue_permutation(tm, dil):
    token = np.arange(tm)
    perm = np.zeros((tm, tm), np.float32)
    perm[token, (token % dil) * (tm // dil) + token // dil] = 1.0
    return jnp.asarray(perm, BF16)


def _head_expansion(n_terms):
    expand = np.zeros((n_terms, LANES, MIX_WIDTH), np.float32)
    for h in range(N_HEADS):
        expand[:, h * (LANES // N_HEADS), h * HEAD_DIM:(h + 1) * HEAD_DIM] = 1.0
    return jnp.asarray(expand.reshape(n_terms * LANES, MIX_WIDTH), BF16)


def _merge_project(x, mod, layer, ya, outs, lses, gain_b, w_out, tm=512):
    b, s, d = x.shape
    tok = lambda w: pl.BlockSpec((1, tm, w), lambda i, j: (i, j, 0))
    res = lambda dil, w: pl.BlockSpec((None, dil, tm // dil, w), lambda i, j: (i, 0, j, 0))
    const = lambda shape: _resident(shape, lambda i, j: (0, 0))
    perms = [_residue_permutation(tm, dil) for dil in DILATIONS[1:]]
    expand = _head_expansion(3)
    return pl.pallas_call(
        _out_kernel,
        grid=(b, s // tm),
        in_specs=[
            tok(d),
            pl.BlockSpec((1, 1, N_MOD, d), lambda i, j: (layer, i, 0, 0)),
            tok(MIX_WIDTH),
            *[res(dil, MIX_WIDTH) for dil in DILATIONS],
            *[res(dil, LANES) for dil in DILATIONS],
            pl.BlockSpec((None, 1, MIX_WIDTH), lambda i, j: (layer, 0, 0)),
            _resident((None, 2 * MIX_WIDTH, d), lambda i, j: (layer, 0, 0)),
            const((tm, tm)), const((tm, tm)), const(expand.shape),
        ],
        out_specs=tok(d),
        out_shape=jax.ShapeDtypeStruct(x.shape, F32),
        scratch_shapes=[pltpu.VMEM((len(DILATIONS), 1, tm, LANES), F32),
                        pltpu.VMEM((1, tm, LANES), F32)],
        compiler_params=_params(2),
        name="merge_out_proj",
    )(x, mod, ya, *outs, *lses, gain_b, w_out, *perms, expand)


def _pair_kv_heads(t, axis):
    n_kv = KV_A_WIDTH // HEAD_DIM
    shape = t.shape
    split = shape[:axis] + (n_kv, N_HEADS // n_kv, HEAD_DIM) + shape[axis + 1:]
    return jnp.swapaxes(t.reshape(split), axis, axis + 1).reshape(shape)


def kernel(x, c, positions, ada_w, ada_b, norm_ffn1, ffn1_wi, ffn1_wo, norm_mix, w_in, sink,
           onorm_a, onorm_b, w_out, norm_ffn2, ffn2_wi, ffn2_wo, final_norm):
    depth = ada_w.shape[0]
    w_in16 = w_in.astype(BF16)
    a_kv, b_q, b_k = MIX_WIDTH + KV_A_WIDTH, MIX_WIDTH + 2 * KV_A_WIDTH, 2 * MIX_WIDTH + 2 * KV_A_WIDTH
    w_in_b = jnp.concatenate([
        _rotary_lane_order(_pair_kv_heads(w_in16[:, :, :MIX_WIDTH], 2)),
        _rotary_lane_order(w_in16[:, :, MIX_WIDTH:a_kv]),
        w_in16[:, :, a_kv:b_q],
        _rotary_lane_order(w_in16[:, :, b_q:b_k + MIX_WIDTH]),
        w_in16[:, :, b_k + MIX_WIDTH:]], axis=2)
    w_out_b = jnp.concatenate([_pair_kv_heads(w_out[:, :MIX_WIDTH].astype(BF16), 1),
                               w_out[:, MIX_WIDTH:].astype(BF16)], axis=1)
    gain_a = _pair_kv_heads(onorm_a, 1).reshape(depth, 1, MIX_WIDTH)
    gain_b = onorm_b.reshape(depth, 1, MIX_WIDTH)
    wi1, wo1 = ffn1_wi.astype(BF16), ffn1_wo.astype(BF16)
    wi2, wo2 = ffn2_wi.astype(BF16), ffn2_wo.astype(BF16)
    g_ffn1 = norm_ffn1.reshape(depth, 1, D_MODEL)
    g_mix = norm_mix.reshape(depth, 1, D_MODEL)
    g_ffn2 = norm_ffn2.reshape(depth, 1, D_MODEL)
    g_final = final_norm.reshape(1, D_MODEL)

    mod = _modulation(c, ada_w, ada_b)
    cos, sin = _rope_tables(positions)
    for l in range(depth):
        x = _ffn(x, mod, l, 0, g_ffn1, wi1, wo1, g_final, False)
        qa, ka, va, qb, kb, vb = _project(x, mod, l, g_mix, cos, sin, w_in_b)
        ya = _attention_a(qa, ka, va, sink, gain_a, l)
        outs, lses = zip(*[_attention_b_branch(qb[i], kb[i], vb[i], w)
                           for i, (w, _) in enumerate(B_BRANCHES)])
        x = _merge_project(x, mod, l, ya, outs, lses, gain_b, w_out_b)
        x = _ffn(x, mod, l, 6, g_ffn2, wi2, wo2, g_final, l == depth - 1)
    return x
```

```python
import functools

import numpy as np
import jax
import jax.numpy as jnp
from jax import lax
from jax.experimental import pallas as pl
from jax.experimental.pallas import tpu as pltpu

D_MODEL = 1024
HEAD_DIM = 64
N_HEADS = 8
MIX_WIDTH = N_HEADS * HEAD_DIM
KV_A_WIDTH = 2 * HEAD_DIM
A_HALF_WINDOW = 128
B_BRANCHES = ((128, 1), (512, 4), (2048, 16))
DILATIONS = tuple(d for _, d in B_BRANCHES)
ROPE_THETA = 500000.0
ROPE_DIM = 16
D_FF = 2816
N_MOD = 9
D_IN = 2304
EPS = 1e-6
NEG_INF = -1e30
LOG2_E = 1.4426950408889634

LANES = 128
Q_ROWS = 128
STAGE_SKEW = 2
B_STEP_ROWS = 2048
F_CHUNK = 256
VMEM_LIMIT = 56 * 1024 * 1024

F32 = jnp.float32
BF16 = jnp.bfloat16


def _params(n_axes):
    return pltpu.CompilerParams(
        dimension_semantics=("arbitrary",) * n_axes, vmem_limit_bytes=VMEM_LIMIT)


def _resident(block_shape, index_map):
    return pl.BlockSpec(block_shape, index_map, pipeline_mode=pl.Buffered(1))


def _norm_mod(x, gain, shift, scale):
    y = x * lax.rsqrt(jnp.mean(x * x, axis=-1, keepdims=True) + EPS)
    return (y * gain) * (1.0 + scale) + shift


def _mod_kernel(c_ref, w_ref, b_ref, o_ref):
    ca = jax.nn.silu(c_ref[...]).astype(BF16)
    o_ref[0] = jnp.dot(ca, w_ref[0].astype(BF16), preferred_element_type=F32) + b_ref[0]


def _modulation(c, ada_w, ada_b):
    depth, d, n = ada_w.shape
    b = c.shape[0]
    tn = n // 8
    out = pl.pallas_call(
        _mod_kernel,
        grid=(depth, n // tn),
        in_specs=[
            pl.BlockSpec((b, d), lambda l, j: (0, 0)),
            pl.BlockSpec((1, d, tn), lambda l, j: (l, 0, j)),
            pl.BlockSpec((1, 1, tn), lambda l, j: (l, 0, j)),
        ],
        out_specs=pl.BlockSpec((1, b, tn), lambda l, j: (l, 0, j)),
        out_shape=jax.ShapeDtypeStruct((depth, b, n), F32),
        compiler_params=_params(2),
        name="adaln_mod",
    )(c, ada_w, ada_b.reshape(depth, 1, n))
    return out.reshape(depth, b, N_MOD, d)


def _ffn_kernel(x_ref, mod_ref, g_ref, wi_ref, wo_ref, fg_ref, o_ref, act_ref, *,
                mod_row, final_norm):
    x = x_ref[0]
    shift = mod_ref[0, 0, mod_row:mod_row + 1, :]
    scale = mod_ref[0, 0, mod_row + 1:mod_row + 2, :]
    gate_res = mod_ref[0, 0, mod_row + 2:mod_row + 3, :]
    hb = _norm_mod(x, g_ref[...], shift, scale).astype(BF16)
    for j in range(D_FF // F_CHUNK):
        lo = j * F_CHUNK
        gate = jnp.dot(hb, wi_ref[:, lo:lo + F_CHUNK], preferred_element_type=F32)
        up = jnp.dot(hb, wi_ref[:, D_FF + lo:D_FF + lo + F_CHUNK], preferred_element_type=F32)
        act_ref[:, lo:lo + F_CHUNK] = (jax.nn.silu(gate) * up).astype(BF16)
    y = jnp.dot(act_ref[...], wo_ref[...], preferred_element_type=F32)
    out = x + (0.5 * gate_res) * y
    if final_norm:
        out = out * lax.rsqrt(jnp.mean(out * out, axis=-1, keepdims=True) + EPS) * fg_ref[...]
    o_ref[0] = out


def _ffn(x, mod, layer, mod_row, gain, wi, wo, final_gain, final_norm, tm=512):
    b, s, d = x.shape
    kern = functools.partial(_ffn_kernel, mod_row=mod_row, final_norm=final_norm)
    return pl.pallas_call(
        kern,
        grid=(b, s // tm),
        in_specs=[
            pl.BlockSpec((1, tm, d), lambda i, j: (i, j, 0)),
            pl.BlockSpec((1, 1, N_MOD, d), lambda i, j: (layer, i, 0, 0)),
            pl.BlockSpec((None, 1, d), lambda i, j: (layer, 0, 0)),
            _resident((None, d, 2 * D_FF), lambda i, j: (layer, 0, 0)),
            _resident((None, D_FF, d), lambda i, j: (layer, 0, 0)),
            pl.BlockSpec((1, d), lambda i, j: (0, 0)),
        ],
        out_specs=pl.BlockSpec((1, tm, d), lambda i, j: (i, j, 0)),
        out_shape=jax.ShapeDtypeStruct(x.shape, F32),
        scratch_shapes=[pltpu.VMEM((tm, D_FF), BF16)],
        compiler_params=_params(2),
        name="ffn",
    )(x, mod, gain, wi, wo, final_gain)


_ROT_HALF = ROPE_DIM // 2


def _rotary_lane_order(t):
    g = t.reshape(t.shape[:-1] + (t.shape[-1] // LANES, LANES))
    a, b = _ROT_HALF, HEAD_DIM
    g = jnp.concatenate([g[..., :a], g[..., b:b + a], g[..., 2 * a:b], g[..., a:2 * a],
                         g[..., b + a:]], axis=-1)
    return g.reshape(t.shape)


def _first_head_lanes(shape):
    lane = lax.broadcasted_iota(jnp.int32, shape, len(shape) - 1)
    return (lane < _ROT_HALF) | ((lane >= ROPE_DIM) & (lane < HEAD_DIM + _ROT_HALF))


def _rope_table_kernel(pos_ref, inv_ref, cos_ref, sin_ref):
    ts = pos_ref.shape[-1]
    ang = inv_ref[...] * pos_ref[0].astype(F32)
    cos_d, sin_d = jnp.cos(ang), jnp.sin(ang)
    plain = HEAD_DIM - ROPE_DIM
    ones, zeros = jnp.ones((plain, ts), F32), jnp.zeros((plain, ts), F32)
    cos_t = jnp.concatenate([cos_d, cos_d, ones, cos_d, cos_d, ones], axis=0)
    sin_t = jnp.concatenate([-sin_d, -sin_d, zeros, sin_d, sin_d, zeros], axis=0)
    cos_ref[0] = cos_t.T
    sin_ref[0] = sin_t.T


def _rope_tables(positions, ts=512):
    b, s = positions.shape
    inv = (ROPE_THETA ** (-np.arange(0, ROPE_DIM, 2, dtype=np.float64) / ROPE_DIM)).astype(np.float32)
    out = pl.BlockSpec((1, ts, LANES), lambda i, j: (i, j, 0))
    return pl.pallas_call(
        _rope_table_kernel,
        grid=(b, s // ts),
        in_specs=[pl.BlockSpec((1, 1, ts), lambda i, j: (i, 0, j)),
                  pl.BlockSpec((_ROT_HALF, 1), lambda i, j: (0, 0))],
        out_specs=[out, out],
        out_shape=[jax.ShapeDtypeStruct((b, s, LANES), F32)] * 2,
        compiler_params=_params(2),
        name="rope_tables",
    )(positions.reshape(b, 1, s), jnp.asarray(inv.reshape(_ROT_HALF, 1)))


def _proj_kernel(x_ref, mod_ref, g_ref, cos_ref, sin_ref, w_ref, *refs):
    qa_ref, ka_ref, va_ref = refs[:3]
    n_dil = len(DILATIONS)
    b_refs = [refs[3 + t * n_dil:3 + (t + 1) * n_dil] for t in range(3)]
    relays = refs[3 + 3 * n_dil:]
    x = x_ref[0]
    tm = x.shape[0]
    hb = _norm_mod(x, g_ref[...], mod_ref[0, 0, 3:4, :], mod_ref[0, 0, 4:5, :]).astype(BF16)

    def rope(t):
        return t * cos_ref[0] + pltpu.roll(t, HEAD_DIM, axis=1) * sin_ref[0]

    proj = jnp.dot(hb, w_ref[...], preferred_element_type=F32)

    def lane_groups(col, width, rotary, mult):
        for g in range(width // LANES):
            tg = proj[:, col + g * LANES:col + (g + 1) * LANES]
            if rotary:
                tg = rope(tg)
            if mult is not None:
                tg = tg * mult
            yield g, tg

    scale = HEAD_DIM ** -0.5 * LOG2_E
    col = 0
    for out_ref, width, rotary, mult in ((qa_ref, MIX_WIDTH, True, scale),
                                         (ka_ref, KV_A_WIDTH, True, None),
                                         (va_ref, KV_A_WIDTH, False, None)):
        for g, tg in lane_groups(col, width, rotary, mult):
            out_ref[0, :, g * LANES:(g + 1) * LANES] = tg.astype(BF16)
        col += width
    n_groups = MIX_WIDTH // LANES
    for t, (outs, rotary, mult) in enumerate(((b_refs[0], True, scale),
                                              (b_refs[1], True, None),
                                              (b_refs[2], False, None))):
        src, dst = relays[2 * t], relays[2 * t + 1]
        for g, tg in lane_groups(col, MIX_WIDTH, rotary, mult):
            src[g] = tg
            outs[0][0, 0, :, g * LANES:(g + 1) * LANES] = tg.astype(BF16)
        for level in range(1, n_dil):
            coarse, fine = DILATIONS[level - 1], DILATIONS[level]
            ratio, n_coarse, n_fine = fine // coarse, tm // coarse, tm // fine
            for rc in range(coarse):
                for q in range(ratio):
                    for g in range(n_groups):
                        rows = src[g, pl.ds(rc * n_coarse + q, n_fine, stride=ratio), :]
                        r = rc + coarse * q
                        if level + 1 < n_dil:
                            dst[g, r * n_fine:(r + 1) * n_fine, :] = rows
                        outs[level][0, r, :, g * LANES:(g + 1) * LANES] = rows.astype(BF16)
            src, dst = dst, src
        col += MIX_WIDTH


def _project(x, mod, layer, gain, cos, sin, w_in, tm=1024):
    b, s, d = x.shape
    tok = lambda w: pl.BlockSpec((1, tm, w), lambda i, j: (i, j, 0))
    res = lambda dil: pl.BlockSpec((1, dil, tm // dil, MIX_WIDTH), lambda i, j: (i, 0, j, 0))
    a_widths = (MIX_WIDTH, KV_A_WIDTH, KV_A_WIDTH)
    out_specs = [tok(w) for w in a_widths] + [res(dil) for _ in range(3) for dil in DILATIONS]
    out_shape = ([jax.ShapeDtypeStruct((b, s, w), BF16) for w in a_widths]
                 + [jax.ShapeDtypeStruct((b, dil, s // dil, MIX_WIDTH), BF16)
                    for _ in range(3) for dil in DILATIONS])
    outs = pl.pallas_call(
        _proj_kernel,
        grid=(b, s // tm),
        in_specs=[
            tok(d),
            pl.BlockSpec((1, 1, N_MOD, d), lambda i, j: (layer, i, 0, 0)),
            pl.BlockSpec((None, 1, d), lambda i, j: (layer, 0, 0)),
            tok(LANES), tok(LANES),
            _resident((None, d, D_IN), lambda i, j: (layer, 0, 0)),
        ],
        out_specs=out_specs,
        out_shape=out_shape,
        scratch_shapes=[pltpu.VMEM((MIX_WIDTH // LANES, tm, LANES), F32)] * 6,
        compiler_params=_params(2),
        name="in_proj_rope",
    )(x, mod, gain, cos, sin, w_in)
    n_dil = len(DILATIONS)
    qa, ka, va = outs[:3]
    qb, kb, vb = (outs[3 + t * n_dil:3 + (t + 1) * n_dil] for t in range(3))
    return qa, ka, va, qb, kb, vb


def _fill_band_bias(bias_ref, hw):
    n_keys = bias_ref.shape[1]
    jj = lax.broadcasted_iota(jnp.int32, (n_keys, 2 * Q_ROWS), 0)
    ii = lax.broadcasted_iota(jnp.int32, (n_keys, 2 * Q_ROWS), 1) & (Q_ROWS - 1)
    rel = jj - ii
    band = (rel >= 0) & (rel <= 2 * hw)
    for c, valid in enumerate((band & (jj >= hw), band, band & (jj < n_keys - hw))):
        bias_ref[c] = jnp.where(valid, 0.0, NEG_INF)


def _key_window(prev_ref, cur_ref, next_ref, sb, hw, cols):
    m = cur_ref.shape[0]
    lo, hi = sb * Q_ROWS - hw, (sb + 1) * Q_ROWS + hw
    parts = [cur_ref[max(lo, 0):min(hi, m), cols]]
    if lo < 0:
        parts.insert(0, prev_ref[:, cols])
    if hi > m:
        parts.append(next_ref[:, cols])
    return parts[0] if len(parts) == 1 else jnp.concatenate(parts, axis=0)


def _band_case(sb, n_sub, step_axis):
    if sb == 0:
        return jnp.where(pl.program_id(step_axis) == 0, 0, 1)
    if sb == n_sub - 1:
        return jnp.where(pl.program_id(step_axis) == pl.num_programs(step_axis) - 1, 2, 1)
    return 1


def _attend_groups(groups, emit):
    n = len(groups)
    state = [None] * n

    def scores(g):
        q_pair, k_win, v_win, bias, sink_row = groups[g]()
        is_lo = _first_head_lanes(q_pair.shape)
        zero = jnp.zeros_like(q_pair)
        q_both = jnp.concatenate(
            [jnp.where(is_lo, q_pair, zero), jnp.where(is_lo, zero, q_pair)], axis=0)
        s = lax.dot_general(k_win, q_both, (((1,), (1,)), ((), ())), preferred_element_type=F32)
        state[g] = (s + bias, v_win, sink_row)

    def softmax(g):
        s, v_win, sink_row = state[g]
        m = jnp.max(s, axis=0, keepdims=True)
        if sink_row is not None:
            m = jnp.maximum(m, sink_row)
        e = jnp.exp2(s - m)
        l_keys = jnp.sum(e, axis=0, keepdims=True)
        denom = l_keys if sink_row is None else l_keys + jnp.exp2(sink_row - m)
        state[g] = (e.astype(BF16), v_win, denom, m + jnp.log2(l_keys))

    def values(g):
        e, v_win, denom, lse = state[g]
        pv = lax.dot_general(v_win, e, (((0,), (0,)), ((), ())), preferred_element_type=F32)
        inv = 1.0 / denom
        out_t = jnp.concatenate(
            [pv[0:HEAD_DIM, 0:Q_ROWS] * inv[:, 0:Q_ROWS],
             pv[HEAD_DIM:2 * HEAD_DIM, Q_ROWS:2 * Q_ROWS] * inv[:, Q_ROWS:]], axis=0)
        state[g] = None
        emit(g, out_t, lse)

    for t in range(n + 2 * STAGE_SKEW):
        if t < n:
            scores(t)
        if STAGE_SKEW <= t < n + STAGE_SKEW:
            softmax(t - STAGE_SKEW)
        if t >= 2 * STAGE_SKEW:
            values(t - 2 * STAGE_SKEW)


def _attn_a_kernel(sink_ref, q_ref, kp_ref, kc_ref, kn_ref, vp_ref, vc_ref, vn_ref, g_ref,
                   y_ref, bias_ref, *, layer):
    hw = A_HALF_WINDOW
    m = q_ref.shape[0]
    n_sub = m // Q_ROWS
    n_pairs = MIX_WIDTH // LANES

    @pl.when((pl.program_id(0) == 0) & (pl.program_id(1) == 0))
    def _():
        _fill_band_bias(bias_ref, hw)

    is_lo_query = lax.broadcasted_iota(jnp.int32, (1, 2 * Q_ROWS), 1) < Q_ROWS
    all_cols = slice(0, KV_A_WIDTH)

    def group(sb, p):
        def load():
            sink_row = jnp.where(is_lo_query, sink_ref[layer, p],
                                 sink_ref[layer, p + n_pairs]) * LOG2_E
            return (q_ref[sb * Q_ROWS:(sb + 1) * Q_ROWS, p * LANES:(p + 1) * LANES],
                    _key_window(kp_ref, kc_ref, kn_ref, sb, hw, all_cols),
                    _key_window(vp_ref, vc_ref, vn_ref, sb, hw, all_cols),
                    bias_ref[_band_case(sb, n_sub, 1)], sink_row)
        return load

    pending = []

    def emit(g, out_t, _):
        pending.append(out_t)
        if len(pending) < n_pairs:
            return
        sb = g // n_pairs
        ssq = sum(jnp.sum(o * o, axis=0, keepdims=True) for o in pending)
        inv = lax.rsqrt(ssq / MIX_WIDTH + EPS)
        for p, o in enumerate(pending):
            y = (o * inv).T * g_ref[:, p * LANES:(p + 1) * LANES]
            y_ref[sb * Q_ROWS:(sb + 1) * Q_ROWS, p * LANES:(p + 1) * LANES] = y.astype(BF16)
        pending.clear()

    _attend_groups([group(sb, p) for sb in range(n_sub) for p in range(n_pairs)], emit)


def _attention_a(qa, ka, va, sink, gain, layer, m=2048):
    b, s, _ = qa.shape
    hw = A_HALF_WINDOW
    per = m // hw
    last = s // hw - 1
    cur = lambda w: pl.BlockSpec((None, m, w), lambda i, j: (i, j, 0))
    prev = pl.BlockSpec((None, hw, KV_A_WIDTH), lambda i, j: (i, jnp.maximum(j * per - 1, 0), 0))
    nxt = pl.BlockSpec((None, hw, KV_A_WIDTH),
                       lambda i, j: (i, jnp.minimum((j + 1) * per, last), 0))
    kern = functools.partial(_attn_a_kernel, layer=layer)
    return pl.pallas_call(
        kern,
        grid=(b, s // m),
        in_specs=[
            pl.BlockSpec(memory_space=pltpu.SMEM),
            cur(MIX_WIDTH),
            prev, cur(KV_A_WIDTH), nxt, prev, cur(KV_A_WIDTH), nxt,
            pl.BlockSpec((None, 1, MIX_WIDTH), lambda i, j: (layer, 0, 0)),
        ],
        out_specs=cur(MIX_WIDTH),
        out_shape=jax.ShapeDtypeStruct((b, s, MIX_WIDTH), BF16),
        scratch_shapes=[pltpu.VMEM((3, Q_ROWS + 2 * hw, 2 * Q_ROWS), F32)],
        compiler_params=_params(2),
        name="attn_a",
    )(sink, qa, ka, ka, ka, va, va, va, gain)


def _attn_b_kernel(q_ref, kp_ref, kc_ref, kn_ref, vp_ref, vc_ref, vn_ref,
                   o_ref, lse_ref, bias_ref, *, hw):
    n_res, m = q_ref.shape[0], q_ref.shape[1]
    n_sub = m // Q_ROWS
    n_pairs = MIX_WIDTH // LANES
    rows_per_head = LANES // N_HEADS

    @pl.when((pl.program_id(0) == 0) & (pl.program_id(1) == 0) & (pl.program_id(2) == 0))
    def _():
        _fill_band_bias(bias_ref, hw)

    def group(r, sb, p):
        cols = slice(p * LANES, (p + 1) * LANES)
        return lambda: (q_ref[r, sb * Q_ROWS:(sb + 1) * Q_ROWS, cols],
                        _key_window(kp_ref.at[r], kc_ref.at[r], kn_ref.at[r], sb, hw, cols),
                        _key_window(vp_ref.at[r], vc_ref.at[r], vn_ref.at[r], sb, hw, cols),
                        bias_ref[_band_case(sb, n_sub, 2)], None)

    lse_rows = []

    def emit(g, out_t, lse):
        r, sb, p = g // (n_sub * n_pairs), (g // n_pairs) % n_sub, g % n_pairs
        rows = slice(sb * Q_ROWS, (sb + 1) * Q_ROWS)
        o_ref[r, rows, p * LANES:(p + 1) * LANES] = out_t.T.astype(BF16)
        lse_rows.extend([lse[:, 0:Q_ROWS], lse[:, Q_ROWS:2 * Q_ROWS]])
        if len(lse_rows) == N_HEADS:
            lse_t = jnp.concatenate(
                [jnp.broadcast_to(v, (rows_per_head, Q_ROWS)) for v in lse_rows], axis=0)
            lse_ref[r, rows, :] = lse_t.T
            lse_rows.clear()

    _attend_groups([group(r, sb, p) for r in range(n_res) for sb in range(n_sub)
                    for p in range(n_pairs)], emit)


def _attention_b_branch(q, k, v, window):
    b, dilation, seq, w = q.shape
    hw = window // (2 * dilation)
    m = min(seq, B_STEP_ROWS)
    n_res = min(dilation, B_STEP_ROWS // m)
    per = m // hw
    last = seq // hw - 1
    cur = lambda width: pl.BlockSpec((None, n_res, m, width), lambda i, r, j: (i, r, j, 0))
    prev = pl.BlockSpec((None, n_res, hw, w),
                        lambda i, r, j: (i, r, jnp.maximum(j * per - 1, 0), 0))
    nxt = pl.BlockSpec((None, n_res, hw, w),
                       lambda i, r, j: (i, r, jnp.minimum((j + 1) * per, last), 0))
    kern = functools.partial(_attn_b_kernel, hw=hw)
    return pl.pallas_call(
        kern,
        grid=(b, dilation // n_res, seq // m),
        in_specs=[cur(w), prev, cur(w), nxt, prev, cur(w), nxt],
        out_specs=[cur(w), cur(LANES)],
        out_shape=[jax.ShapeDtypeStruct((b, dilation, seq, w), BF16),
                   jax.ShapeDtypeStruct((b, dilation, seq, LANES), F32)],
        scratch_shapes=[pltpu.VMEM((3, Q_ROWS + 2 * hw, 2 * Q_ROWS), F32)],
        compiler_params=_params(3),
        name=f"attn_b_d{dilation}",
    )(q, k, k, k, v, v, v)


def _split_bf16(v):
    hi = v.astype(BF16)
    rest = v - hi.astype(F32)
    mid = rest.astype(BF16)
    return hi, mid, (rest - mid.astype(F32)).astype(BF16)


def _out_kernel(x_ref, mod_ref, ya_ref, o1_ref, o2_ref, o3_ref, l1_ref, l2_ref, l3_ref,
                gb_ref, w_ref, perm2_ref, perm3_ref, expand_ref, o_ref, relay_l, tmp_l):
    x = x_ref[0]
    tm = x.shape[0]

    def natural_order(src_ref, dil, dst, tmp, n_lane_groups):
        level = DILATIONS.index(dil)
        bufs = (tmp, dst)
        for lv in range(level, 0, -1):
            coarse, fine = DILATIONS[lv - 1], DILATIONS[lv]
            ratio, n_coarse, n_fine = fine // coarse, tm // coarse, tm // fine
            out = bufs[lv % 2]
            for rc in range(coarse):
                for q in range(ratio):
                    r = rc + coarse * q
                    for g in range(n_lane_groups):
                        if lv == level:
                            rows = src_ref[r, :, g * LANES:(g + 1) * LANES]
                        else:
                            rows = bufs[(lv + 1) % 2][g, r * n_fine:(r + 1) * n_fine, :]
                        out[g, pl.ds(rc * n_coarse + q, n_fine, stride=ratio), :] = rows

    lses = []
    for i, (l_ref, dil) in enumerate(zip((l1_ref, l2_ref, l3_ref), DILATIONS)):
        if dil == 1:
            lses.append(l_ref[0])
        else:
            natural_order(l_ref, dil, relay_l.at[i], tmp_l, 1)
            lses.append(relay_l[i, 0])
    mx = jnp.maximum(jnp.maximum(lses[0], lses[1]), lses[2])
    es = [jnp.exp2(l - mx) for l in lses]
    den = es[0] + es[1] + es[2]

    def head_weight(e):
        terms = jnp.concatenate(_split_bf16(e / den), axis=1)
        return jnp.dot(terms, expand_ref[...], preferred_element_type=F32)

    o1 = o1_ref[0].astype(F32)
    o2 = jnp.dot(perm2_ref[...], o2_ref[...].reshape(tm, MIX_WIDTH), preferred_element_type=F32)
    o3 = jnp.dot(perm3_ref[...], o3_ref[...].reshape(tm, MIX_WIDTH), preferred_element_type=F32)
    ob = o1 + head_weight(es[1]) * (o2 - o1) + head_weight(es[2]) * (o3 - o1)
    inv = lax.rsqrt(jnp.mean(ob * ob, axis=1, keepdims=True) + EPS)
    yb = ((ob * inv) * gb_ref[...]).astype(BF16)
    mix = jnp.dot(ya_ref[0], w_ref[0:MIX_WIDTH, :], preferred_element_type=F32)
    mix = mix + jnp.dot(yb, w_ref[MIX_WIDTH:2 * MIX_WIDTH, :], preferred_element_type=F32)
    o_ref[0] = x + mod_ref[0, 0, 5:6, :] * mix


def _residue_permutation(tm, dil):
    token = np.arange(tm)
    perm = np.zeros((tm, tm), np.float32)
    perm[token, (token % dil) * (tm // dil) + token // dil] = 1.0
    return jnp.asarray(perm, BF16)


def _head_expansion(n_terms):
    expand = np.zeros((n_terms, LANES, MIX_WIDTH), np.float32)
    for h in range(N_HEADS):
        expand[:, h * (LANES // N_HEADS), h * HEAD_DIM:(h + 1) * HEAD_DIM] = 1.0
    return jnp.asarray(expand.reshape(n_terms * LANES, MIX_WIDTH), BF16)


def _merge_project(x, mod, layer, ya, outs, lses, gain_b, w_out, tm=512):
    b, s, d = x.shape
    tok = lambda w: pl.BlockSpec((1, tm, w), lambda i, j: (i, j, 0))
    res = lambda dil, w: pl.BlockSpec((None, dil, tm // dil, w), lambda i, j: (i, 0, j, 0))
    const = lambda shape: _resident(shape, lambda i, j: (0, 0))
    perms = [_residue_permutation(tm, dil) for dil in DILATIONS[1:]]
    expand = _head_expansion(3)
    return pl.pallas_call(
        _out_kernel,
        grid=(b, s // tm),
        in_specs=[
            tok(d),
            pl.BlockSpec((1, 1, N_MOD, d), lambda i, j: (layer, i, 0, 0)),
            tok(MIX_WIDTH),
            *[res(dil, MIX_WIDTH) for dil in DILATIONS],
            *[res(dil, LANES) for dil in DILATIONS],
            pl.BlockSpec((None, 1, MIX_WIDTH), lambda i, j: (layer, 0, 0)),
            _resident((None, 2 * MIX_WIDTH, d), lambda i, j: (layer, 0, 0)),
            const((tm, tm)), const((tm, tm)), const(expand.shape),
        ],
        out_specs=tok(d),
        out_shape=jax.ShapeDtypeStruct(x.shape, F32),
        scratch_shapes=[pltpu.VMEM((len(DILATIONS), 1, tm, LANES), F32),
                        pltpu.VMEM((1, tm, LANES), F32)],
        compiler_params=_params(2),
        name="merge_out_proj",
    )(x, mod, ya, *outs, *lses, gain_b, w_out, *perms, expand)


def _pair_kv_heads(t, axis):
    n_kv = KV_A_WIDTH // HEAD_DIM
    shape = t.shape
    split = shape[:axis] + (n_kv, N_HEADS // n_kv, HEAD_DIM) + shape[axis + 1:]
    return jnp.swapaxes(t.reshape(split), axis, axis + 1).reshape(shape)


def kernel(x, c, positions, ada_w, ada_b, norm_ffn1, ffn1_wi, ffn1_wo, norm_mix, w_in, sink,
           onorm_a, onorm_b, w_out, norm_ffn2, ffn2_wi, ffn2_wo, final_norm):
    depth = ada_w.shape[0]
    w_in16 = w_in.astype(BF16)
    a_kv, b_q, b_k = MIX_WIDTH + KV_A_WIDTH, MIX_WIDTH + 2 * KV_A_WIDTH, 2 * MIX_WIDTH + 2 * KV_A_WIDTH
    w_in_b = jnp.concatenate([
        _rotary_lane_order(_pair_kv_heads(w_in16[:, :, :MIX_WIDTH], 2)),
        _rotary_lane_order(w_in16[:, :, MIX_WIDTH:a_kv]),
        w_in16[:, :, a_kv:b_q],
        _rotary_lane_order(w_in16[:, :, b_q:b_k + MIX_WIDTH]),
        w_in16[:, :, b_k + MIX_WIDTH:]], axis=2)
    w_out_b = jnp.concatenate([_pair_kv_heads(w_out[:, :MIX_WIDTH].astype(BF16), 1),
                               w_out[:, MIX_WIDTH:].astype(BF16)], axis=1)
    gain_a = _pair_kv_heads(onorm_a, 1).reshape(depth, 1, MIX_WIDTH)
    gain_b = onorm_b.reshape(depth, 1, MIX_WIDTH)
    wi1, wo1 = ffn1_wi.astype(BF16), ffn1_wo.astype(BF16)
    wi2, wo2 = ffn2_wi.astype(BF16), ffn2_wo.astype(BF16)
    g_ffn1 = norm_ffn1.reshape(depth, 1, D_MODEL)
    g_mix = norm_mix.reshape(depth, 1, D_MODEL)
    g_ffn2 = norm_ffn2.reshape(depth, 1, D_MODEL)
    g_final = final_norm.reshape(1, D_MODEL)

    mod = _modulation(c, ada_w, ada_b)
    cos, sin = _rope_tables(positions)
    for l in range(depth):
        x = _ffn(x, mod, l, 0, g_ffn1, wi1, wo1, g_final, False)
        qa, ka, va, qb, kb, vb = _project(x, mod, l, g_mix, cos, sin, w_in_b)
        ya = _attention_a(qa, ka, va, sink, gain_a, l)
        outs, lses = zip(*[_attention_b_branch(qb[i], kb[i], vb[i], w)
                           for i, (w, _) in enumerate(B_BRANCHES)])
        x = _merge_project(x, mod, l, ya, outs, lses, gain_b, w_out_b)
        x = _ffn(x, mod, l, 6, g_ffn2, wi2, wo2, g_final, l == depth - 1)
    return x
```

```python
import functools

import numpy as np
import jax
import jax.numpy as jnp
from jax import lax
from jax.experimental import pallas as pl
from jax.experimental.pallas import tpu as pltpu

D_MODEL = 1024
HEAD_DIM = 64
N_HEADS = 8
MIX_WIDTH = N_HEADS * HEAD_DIM
KV_A_WIDTH = 2 * HEAD_DIM
A_HALF_WINDOW = 128
B_BRANCHES = ((128, 1), (512, 4), (2048, 16))
DILATIONS = tuple(d for _, d in B_BRANCHES)
ROPE_THETA = 500000.0
ROPE_DIM = 16
D_FF = 2816
N_MOD = 9
D_IN = 2304
EPS = 1e-6
NEG_INF = -1e30
LOG2_E = 1.4426950408889634

LANES = 128
Q_ROWS = 128
STAGE_SKEW = 2
B_STEP_ROWS = 2048
F_CHUNK = 256
VMEM_LIMIT = 56 * 1024 * 1024

F32 = jnp.float32
BF16 = jnp.bfloat16


def _params(n_axes):
    return pltpu.CompilerParams(
        dimension_semantics=("arbitrary",) * n_axes, vmem_limit_bytes=VMEM_LIMIT)


def _resident(block_shape, index_map):
    return pl.BlockSpec(block_shape, index_map, pipeline_mode=pl.Buffered(1))


def _norm_mod(x, gain, shift, scale):
    y = x * lax.rsqrt(jnp.mean(x * x, axis=-1, keepdims=True) + EPS)
    return (y * gain) * (1.0 + scale) + shift


def _mod_kernel(c_ref, w_ref, b_ref, o_ref):
    ca = jax.nn.silu(c_ref[...]).astype(BF16)
    o_ref[0] = jnp.dot(ca, w_ref[0].astype(BF16), preferred_element_type=F32) + b_ref[0]


def _modulation(c, ada_w, ada_b):
    depth, d, n = ada_w.shape
    b = c.shape[0]
    tn = n // 8
    out = pl.pallas_call(
        _mod_kernel,
        grid=(depth, n // tn),
        in_specs=[
            pl.BlockSpec((b, d), lambda l, j: (0, 0)),
            pl.BlockSpec((1, d, tn), lambda l, j: (l, 0, j)),
            pl.BlockSpec((1, 1, tn), lambda l, j: (l, 0, j)),
        ],
        out_specs=pl.BlockSpec((1, b, tn), lambda l, j: (l, 0, j)),
        out_shape=jax.ShapeDtypeStruct((depth, b, n), F32),
        compiler_params=_params(2),
        name="adaln_mod",
    )(c, ada_w, ada_b.reshape(depth, 1, n))
    return out.reshape(depth, b, N_MOD, d)


def _ffn_step(x, mod_ref, mod_row, g_ref, wi_ref, wo_ref, act_ref):
    shift = mod_ref[0, 0, mod_row:mod_row + 1, :]
    scale = mod_ref[0, 0, mod_row + 1:mod_row + 2, :]
    gate_res = mod_ref[0, 0, mod_row + 2:mod_row + 3, :]
    hb = _norm_mod(x, g_ref[...], shift, scale).astype(BF16)
    for j in range(D_FF // F_CHUNK):
        lo = j * F_CHUNK
        gate = jnp.dot(hb, wi_ref[:, lo:lo + F_CHUNK], preferred_element_type=F32)
        up = jnp.dot(hb, wi_ref[:, D_FF + lo:D_FF + lo + F_CHUNK], preferred_element_type=F32)
        act_ref[:, lo:lo + F_CHUNK] = (jax.nn.silu(gate) * up).astype(BF16)
    y = jnp.dot(act_ref[...], wo_ref[...], preferred_element_type=F32)
    return x + (0.5 * gate_res) * y


def _ffn_specs(layer, d):
    return [pl.BlockSpec((None, 1, d), lambda i, j: (layer, 0, 0)),
            _resident((None, d, 2 * D_FF), lambda i, j: (layer, 0, 0)),
            _resident((None, D_FF, d), lambda i, j: (layer, 0, 0))]


_ROT_HALF = ROPE_DIM // 2


def _rotary_lane_order(t):
    g = t.reshape(t.shape[:-1] + (t.shape[-1] // LANES, LANES))
    a, b = _ROT_HALF, HEAD_DIM
    g = jnp.concatenate([g[..., :a], g[..., b:b + a], g[..., 2 * a:b], g[..., a:2 * a],
                         g[..., b + a:]], axis=-1)
    return g.reshape(t.shape)


def _first_head_lanes(shape):
    lane = lax.broadcasted_iota(jnp.int32, shape, len(shape) - 1)
    return (lane < _ROT_HALF) | ((lane >= ROPE_DIM) & (lane < HEAD_DIM + _ROT_HALF))


def _rope_table_kernel(pos_ref, inv_ref, cos_ref, sin_ref):
    ts = pos_ref.shape[-1]
    ang = inv_ref[...] * pos_ref[0].astype(F32)
    cos_d, sin_d = jnp.cos(ang), jnp.sin(ang)
    plain = HEAD_DIM - ROPE_DIM
    ones, zeros = jnp.ones((plain, ts), F32), jnp.zeros((plain, ts), F32)
    cos_t = jnp.concatenate([cos_d, cos_d, ones, cos_d, cos_d, ones], axis=0)
    sin_t = jnp.concatenate([-sin_d, -sin_d, zeros, sin_d, sin_d, zeros], axis=0)
    cos_ref[0] = cos_t.T
    sin_ref[0] = sin_t.T


def _rope_tables(positions, ts=4096):
    b, s = positions.shape
    inv = (ROPE_THETA ** (-np.arange(0, ROPE_DIM, 2, dtype=np.float64) / ROPE_DIM)).astype(np.float32)
    out = pl.BlockSpec((1, ts, LANES), lambda i, j: (i, j, 0))
    return pl.pallas_call(
        _rope_table_kernel,
        grid=(b, s // ts),
        in_specs=[pl.BlockSpec((1, 1, ts), lambda i, j: (i, 0, j)),
                  pl.BlockSpec((_ROT_HALF, 1), lambda i, j: (0, 0))],
        out_specs=[out, out],
        out_shape=[jax.ShapeDtypeStruct((b, s, LANES), F32)] * 2,
        compiler_params=_params(2),
        name="rope_tables",
    )(positions.reshape(b, 1, s), jnp.asarray(inv.reshape(_ROT_HALF, 1)))


def _project_step(x, mod_ref, g_ref, cos_ref, sin_ref, w_ref, a_refs, b_refs, relays):
    qa_ref, ka_ref, va_ref = a_refs
    n_dil = len(DILATIONS)
    tm = x.shape[0]
    hb = _norm_mod(x, g_ref[...], mod_ref[0, 0, 3:4, :], mod_ref[0, 0, 4:5, :]).astype(BF16)

    def rope(t):
        return t * cos_ref[0] + pltpu.roll(t, HEAD_DIM, axis=1) * sin_ref[0]

    proj = jnp.dot(hb, w_ref[...], preferred_element_type=F32)

    def lane_groups(col, width, rotary, mult):
        for g in range(width // LANES):
            tg = proj[:, col + g * LANES:col + (g + 1) * LANES]
            if rotary:
                tg = rope(tg)
            if mult is not None:
                tg = tg * mult
            yield g, tg

    scale = HEAD_DIM ** -0.5 * LOG2_E
    col = 0
    for out_ref, width, rotary, mult in ((qa_ref, MIX_WIDTH, True, scale),
                                         (ka_ref, KV_A_WIDTH, True, None),
                                         (va_ref, KV_A_WIDTH, False, None)):
        for g, tg in lane_groups(col, width, rotary, mult):
            out_ref[0, :, g * LANES:(g + 1) * LANES] = tg.astype(BF16)
        col += width
    n_groups = MIX_WIDTH // LANES
    for t, (outs, rotary, mult) in enumerate(((b_refs[0], True, scale),
                                              (b_refs[1], True, None),
                                              (b_refs[2], False, None))):
        src, dst = relays[2 * t], relays[2 * t + 1]
        for g, tg in lane_groups(col, MIX_WIDTH, rotary, mult):
            src[g] = tg
            outs[0][0, 0, :, g * LANES:(g + 1) * LANES] = tg.astype(BF16)
        for level in range(1, n_dil):
            coarse, fine = DILATIONS[level - 1], DILATIONS[level]
            ratio, n_coarse, n_fine = fine // coarse, tm // coarse, tm // fine
            for rc in range(coarse):
                for q in range(ratio):
                    for g in range(n_groups):
                        rows = src[g, pl.ds(rc * n_coarse + q, n_fine, stride=ratio), :]
                        r = rc + coarse * q
                        if level + 1 < n_dil:
                            dst[g, r * n_fine:(r + 1) * n_fine, :] = rows
                        outs[level][0, r, :, g * LANES:(g + 1) * LANES] = rows.astype(BF16)
            src, dst = dst, src
        col += MIX_WIDTH


def _ffn_proj_kernel(x_ref, mod_ref, g1_ref, wi_ref, wo_ref, gm_ref, cos_ref, sin_ref, w_ref,
                     xo_ref, *refs):
    n_dil = len(DILATIONS)
    b_refs = [refs[3 + t * n_dil:3 + (t + 1) * n_dil] for t in range(3)]
    act_ref = refs[3 + 3 * n_dil]
    x = _ffn_step(x_ref[0], mod_ref, 0, g1_ref, wi_ref, wo_ref, act_ref)
    xo_ref[0] = x
    _project_step(x, mod_ref, gm_ref, cos_ref, sin_ref, w_ref, refs[:3], b_refs,
                  refs[4 + 3 * n_dil:])


def _ffn_project(x, mod, layer, gain_ffn, wi, wo, gain_mix, cos, sin, w_in, tm=512):
    b, s, d = x.shape
    tok = lambda w: pl.BlockSpec((1, tm, w), lambda i, j: (i, j, 0))
    res = lambda dil: pl.BlockSpec((1, dil, tm // dil, MIX_WIDTH), lambda i, j: (i, 0, j, 0))
    a_widths = (MIX_WIDTH, KV_A_WIDTH, KV_A_WIDTH)
    out_specs = ([tok(d)] + [tok(w) for w in a_widths]
                 + [res(dil) for _ in range(3) for dil in DILATIONS])
    out_shape = ([jax.ShapeDtypeStruct(x.shape, F32)]
                 + [jax.ShapeDtypeStruct((b, s, w), BF16) for w in a_widths]
                 + [jax.ShapeDtypeStruct((b, dil, s // dil, MIX_WIDTH), BF16)
                    for _ in range(3) for dil in DILATIONS])
    outs = pl.pallas_call(
        _ffn_proj_kernel,
        grid=(b, s // tm),
        in_specs=[
            tok(d),
            pl.BlockSpec((1, 1, N_MOD, d), lambda i, j: (layer, i, 0, 0)),
            *_ffn_specs(layer, d),
            pl.BlockSpec((None, 1, d), lambda i, j: (layer, 0, 0)),
            tok(LANES), tok(LANES),
            _resident((None, d, D_IN), lambda i, j: (layer, 0, 0)),
        ],
        out_specs=out_specs,
        out_shape=out_shape,
        scratch_shapes=[pltpu.VMEM((tm, D_FF), BF16)]
        + [pltpu.VMEM((MIX_WIDTH // LANES, tm, LANES), F32)] * 6,
        compiler_params=_params(2),
        name="ffn_in_proj",
    )(x, mod, gain_ffn, wi, wo, gain_mix, cos, sin, w_in)
    n_dil = len(DILATIONS)
    x, qa, ka, va = outs[:4]
    qb, kb, vb = (outs[4 + t * n_dil:4 + (t + 1) * n_dil] for t in range(3))
    return x, qa, ka, va, qb, kb, vb


def _fill_band_bias(bias_ref, hw):
    n_keys = bias_ref.shape[1]
    jj = lax.broadcasted_iota(jnp.int32, (n_keys, 2 * Q_ROWS), 0)
    ii = lax.broadcasted_iota(jnp.int32, (n_keys, 2 * Q_ROWS), 1) & (Q_ROWS - 1)
    rel = jj - ii
    band = (rel >= 0) & (rel <= 2 * hw)
    for c, valid in enumerate((band & (jj >= hw), band, band & (jj < n_keys - hw))):
        bias_ref[c] = jnp.where(valid, 0.0, NEG_INF)


def _key_window(prev_ref, cur_ref, next_ref, sb, hw, cols):
    m = cur_ref.shape[0]
    lo, hi = sb * Q_ROWS - hw, (sb + 1) * Q_ROWS + hw
    parts = [cur_ref[max(lo, 0):min(hi, m), cols]]
    if lo < 0:
        parts.insert(0, prev_ref[:, cols])
    if hi > m:
        parts.append(next_ref[:, cols])
    return parts[0] if len(parts) == 1 else jnp.concatenate(parts, axis=0)


def _band_case(sb, n_sub, step_axis):
    if sb == 0:
        return jnp.where(pl.program_id(step_axis) == 0, 0, 1)
    if sb == n_sub - 1:
        return jnp.where(pl.program_id(step_axis) == pl.num_programs(step_axis) - 1, 2, 1)
    return 1


def _attend_groups(groups, emit):
    n = len(groups)
    state = [None] * n

    def scores(g):
        q_pair, k_win, v_win, bias, sink_row = groups[g]()
        is_lo = _first_head_lanes(q_pair.shape)
        zero = jnp.zeros_like(q_pair)
        q_both = jnp.concatenate(
            [jnp.where(is_lo, q_pair, zero), jnp.where(is_lo, zero, q_pair)], axis=0)
        s = lax.dot_general(k_win, q_both, (((1,), (1,)), ((), ())), preferred_element_type=F32)
        state[g] = (s + bias, v_win, sink_row)

    def softmax(g):
        s, v_win, sink_row = state[g]
        m = jnp.max(s, axis=0, keepdims=True)
        if sink_row is not None:
            m = jnp.maximum(m, sink_row)
        e = jnp.exp2(s - m)
        l_keys = jnp.sum(e, axis=0, keepdims=True)
        denom = l_keys if sink_row is None else l_keys + jnp.exp2(sink_row - m)
        state[g] = (e.astype(BF16), v_win, denom, m + jnp.log2(l_keys))

    def values(g):
        e, v_win, denom, lse = state[g]
        pv = lax.dot_general(v_win, e, (((0,), (0,)), ((), ())), preferred_element_type=F32)
        inv = 1.0 / denom
        out_t = jnp.concatenate(
            [pv[0:HEAD_DIM, 0:Q_ROWS] * inv[:, 0:Q_ROWS],
             pv[HEAD_DIM:2 * HEAD_DIM, Q_ROWS:2 * Q_ROWS] * inv[:, Q_ROWS:]], axis=0)
        state[g] = None
        emit(g, out_t, lse)

    for t in range(n + 2 * STAGE_SKEW):
        if t < n:
            scores(t)
        if STAGE_SKEW <= t < n + STAGE_SKEW:
            softmax(t - STAGE_SKEW)
        if t >= 2 * STAGE_SKEW:
            values(t - 2 * STAGE_SKEW)


def _attn_a_kernel(sink_ref, q_ref, kp_ref, kc_ref, kn_ref, vp_ref, vc_ref, vn_ref, g_ref,
                   y_ref, bias_ref, *, layer):
    hw = A_HALF_WINDOW
    m = q_ref.shape[0]
    n_sub = m // Q_ROWS
    n_pairs = MIX_WIDTH // LANES

    @pl.when((pl.program_id(0) == 0) & (pl.program_id(1) == 0))
    def _():
        _fill_band_bias(bias_ref, hw)

    is_lo_query = lax.broadcasted_iota(jnp.int32, (1, 2 * Q_ROWS), 1) < Q_ROWS
    all_cols = slice(0, KV_A_WIDTH)

    def group(sb, p):
        def load():
            sink_row = jnp.where(is_lo_query, sink_ref[layer, p],
                                 sink_ref[layer, p + n_pairs]) * LOG2_E
            return (q_ref[sb * Q_ROWS:(sb + 1) * Q_ROWS, p * LANES:(p + 1) * LANES],
                    _key_window(kp_ref, kc_ref, kn_ref, sb, hw, all_cols),
                    _key_window(vp_ref, vc_ref, vn_ref, sb, hw, all_cols),
                    bias_ref[_band_case(sb, n_sub, 1)], sink_row)
        return load

    pending = []

    def emit(g, out_t, _):
        pending.append(out_t)
        if len(pending) < n_pairs:
            return
        sb = g // n_pairs
        ssq = sum(jnp.sum(o * o, axis=0, keepdims=True) for o in pending)
        inv = lax.rsqrt(ssq / MIX_WIDTH + EPS)
        for p, o in enumerate(pending):
            y = (o * inv).T * g_ref[:, p * LANES:(p + 1) * LANES]
            y_ref[sb * Q_ROWS:(sb + 1) * Q_ROWS, p * LANES:(p + 1) * LANES] = y.astype(BF16)
        pending.clear()

    _attend_groups([group(sb, p) for sb in range(n_sub) for p in range(n_pairs)], emit)


def _attention_a(qa, ka, va, sink, gain, layer, m=2048):
    b, s, _ = qa.shape
    hw = A_HALF_WINDOW
    per = m // hw
    last = s // hw - 1
    cur = lambda w: pl.BlockSpec((None, m, w), lambda i, j: (i, j, 0))
    prev = pl.BlockSpec((None, hw, KV_A_WIDTH), lambda i, j: (i, jnp.maximum(j * per - 1, 0), 0))
    nxt = pl.BlockSpec((None, hw, KV_A_WIDTH),
                       lambda i, j: (i, jnp.minimum((j + 1) * per, last), 0))
    kern = functools.partial(_attn_a_kernel, layer=layer)
    return pl.pallas_call(
        kern,
        grid=(b, s // m),
        in_specs=[
            pl.BlockSpec(memory_space=pltpu.SMEM),
            cur(MIX_WIDTH),
            prev, cur(KV_A_WIDTH), nxt, prev, cur(KV_A_WIDTH), nxt,
            pl.BlockSpec((None, 1, MIX_WIDTH), lambda i, j: (layer, 0, 0)),
        ],
        out_specs=cur(MIX_WIDTH),
        out_shape=jax.ShapeDtypeStruct((b, s, MIX_WIDTH), BF16),
        scratch_shapes=[pltpu.VMEM((3, Q_ROWS + 2 * hw, 2 * Q_ROWS), F32)],
        compiler_params=_params(2),
        name="attn_a",
    )(sink, qa, ka, ka, ka, va, va, va, gain)


def _attn_b_kernel(q_ref, kp_ref, kc_ref, kn_ref, vp_ref, vc_ref, vn_ref,
                   o_ref, lse_ref, bias_ref, *, hw):
    n_res, m = q_ref.shape[0], q_ref.shape[1]
    n_sub = m // Q_ROWS
    n_pairs = MIX_WIDTH // LANES
    rows_per_head = LANES // N_HEADS

    @pl.when((pl.program_id(0) == 0) & (pl.program_id(1) == 0) & (pl.program_id(2) == 0))
    def _():
        _fill_band_bias(bias_ref, hw)

    def group(r, sb, p):
        cols = slice(p * LANES, (p + 1) * LANES)
        return lambda: (q_ref[r, sb * Q_ROWS:(sb + 1) * Q_ROWS, cols],
                        _key_window(kp_ref.at[r], kc_ref.at[r], kn_ref.at[r], sb, hw, cols),
                        _key_window(vp_ref.at[r], vc_ref.at[r], vn_ref.at[r], sb, hw, cols),
                        bias_ref[_band_case(sb, n_sub, 2)], None)

    lse_rows = []

    def emit(g, out_t, lse):
        r, sb, p = g // (n_sub * n_pairs), (g // n_pairs) % n_sub, g % n_pairs
        rows = slice(sb * Q_ROWS, (sb + 1) * Q_ROWS)
        o_ref[r, rows, p * LANES:(p + 1) * LANES] = out_t.T.astype(BF16)
        lse_rows.extend([lse[:, 0:Q_ROWS], lse[:, Q_ROWS:2 * Q_ROWS]])
        if len(lse_rows) == N_HEADS:
            lse_t = jnp.concatenate(
                [jnp.broadcast_to(v, (rows_per_head, Q_ROWS)) for v in lse_rows], axis=0)
            lse_ref[r, rows, :] = lse_t.T
            lse_rows.clear()

    _attend_groups([group(r, sb, p) for r in range(n_res) for sb in range(n_sub)
                    for p in range(n_pairs)], emit)


def _attention_b_branch(q, k, v, window):
    b, dilation, seq, w = q.shape
    hw = window // (2 * dilation)
    m = min(seq, B_STEP_ROWS)
    n_res = min(dilation, B_STEP_ROWS // m)
    per = m // hw
    last = seq // hw - 1
    cur = lambda width: pl.BlockSpec((None, n_res, m, width), lambda i, r, j: (i, r, j, 0))
    prev = pl.BlockSpec((None, n_res, hw, w),
                        lambda i, r, j: (i, r, jnp.maximum(j * per - 1, 0), 0))
    nxt = pl.BlockSpec((None, n_res, hw, w),
                       lambda i, r, j: (i, r, jnp.minimum((j + 1) * per, last), 0))
    kern = functools.partial(_attn_b_kernel, hw=hw)
    return pl.pallas_call(
        kern,
        grid=(b, dilation // n_res, seq // m),
        in_specs=[cur(w), prev, cur(w), nxt, prev, cur(w), nxt],
        out_specs=[cur(w), cur(LANES)],
        out_shape=[jax.ShapeDtypeStruct((b, dilation, seq, w), BF16),
                   jax.ShapeDtypeStruct((b, dilation, seq, LANES), F32)],
        scratch_shapes=[pltpu.VMEM((3, Q_ROWS + 2 * hw, 2 * Q_ROWS), F32)],
        compiler_params=_params(3),
        name=f"attn_b_d{dilation}",
    )(q, k, k, k, v, v, v)


def _split_bf16(v):
    hi = v.astype(BF16)
    rest = v - hi.astype(F32)
    mid = rest.astype(BF16)
    return hi, mid, (rest - mid.astype(F32)).astype(BF16)


def _merge_ffn_kernel(x_ref, mod_ref, ya_ref, o1_ref, o2_ref, o3_ref, l1_ref, l2_ref, l3_ref,
                      gb_ref, w_ref, perm2_ref, perm3_ref, expand_ref, g2_ref, wi_ref, wo_ref,
                      fg_ref, o_ref, relay_l, tmp_l, act_ref, *, final_norm):
    x = x_ref[0]
    tm = x.shape[0]

    def natural_order(src_ref, dil, dst, tmp, n_lane_groups):
        level = DILATIONS.index(dil)
        bufs = (tmp, dst)
        for lv in range(level, 0, -1):
            coarse, fine = DILATIONS[lv - 1], DILATIONS[lv]
            ratio, n_coarse, n_fine = fine // coarse, tm // coarse, tm // fine
            out = bufs[lv % 2]
            for rc in range(coarse):
                for q in range(ratio):
                    r = rc + coarse * q
                    for g in range(n_lane_groups):
                        if lv == level:
                            rows = src_ref[r, :, g * LANES:(g + 1) * LANES]
                        else:
                            rows = bufs[(lv + 1) % 2][g, r * n_fine:(r + 1) * n_fine, :]
                        out[g, pl.ds(rc * n_coarse + q, n_fine, stride=ratio), :] = rows

    lses = []
    for i, (l_ref, dil) in enumerate(zip((l1_ref, l2_ref, l3_ref), DILATIONS)):
        if dil == 1:
            lses.append(l_ref[0])
        else:
            natural_order(l_ref, dil, relay_l.at[i], tmp_l, 1)
            lses.append(relay_l[i, 0])
    mx = jnp.maximum(jnp.maximum(lses[0], lses[1]), lses[2])
    es = [jnp.exp2(l - mx) for l in lses]
    den = es[0] + es[1] + es[2]

    def head_weight(e):
        terms = jnp.concatenate(_split_bf16(e / den), axis=1)
        return jnp.dot(terms, expand_ref[...], preferred_element_type=F32)

    o1 = o1_ref[0].astype(F32)
    o2 = jnp.dot(perm2_ref[...], o2_ref[...].reshape(tm, MIX_WIDTH), preferred_element_type=F32)
    o3 = jnp.dot(perm3_ref[...], o3_ref[...].reshape(tm, MIX_WIDTH), preferred_element_type=F32)
    ob = o1 + head_weight(es[1]) * (o2 - o1) + head_weight(es[2]) * (o3 - o1)
    inv = lax.rsqrt(jnp.mean(ob * ob, axis=1, keepdims=True) + EPS)
    yb = ((ob * inv) * gb_ref[...]).astype(BF16)
    mix = jnp.dot(ya_ref[0], w_ref[0:MIX_WIDTH, :], preferred_element_type=F32)
    mix = mix + jnp.dot(yb, w_ref[MIX_WIDTH:2 * MIX_WIDTH, :], preferred_element_type=F32)
    x = x + mod_ref[0, 0, 5:6, :] * mix
    out = _ffn_step(x, mod_ref, 6, g2_ref, wi_ref, wo_ref, act_ref)
    if final_norm:
        out = out * lax.rsqrt(jnp.mean(out * out, axis=-1, keepdims=True) + EPS) * fg_ref[...]
    o_ref[0] = out


def _residue_permutation(tm, dil):
    token = np.arange(tm)
    perm = np.zeros((tm, tm), np.float32)
    perm[token, (token % dil) * (tm // dil) + token // dil] = 1.0
    return jnp.asarray(perm, BF16)


def _head_expansion(n_terms):
    expand = np.zeros((n_terms, LANES, MIX_WIDTH), np.float32)
    for h in range(N_HEADS):
        expand[:, h * (LANES // N_HEADS), h * HEAD_DIM:(h + 1) * HEAD_DIM] = 1.0
    return jnp.asarray(expand.reshape(n_terms * LANES, MIX_WIDTH), BF16)


def _merge_ffn(x, mod, layer, ya, outs, lses, gain_b, w_out, gain_ffn, wi, wo, final_gain,
               final_norm, tm=512):
    b, s, d = x.shape
    tok = lambda w: pl.BlockSpec((1, tm, w), lambda i, j: (i, j, 0))
    res = lambda dil, w: pl.BlockSpec((None, dil, tm // dil, w), lambda i, j: (i, 0, j, 0))
    const = lambda shape: _resident(shape, lambda i, j: (0, 0))
    perms = [_residue_permutation(tm, dil) for dil in DILATIONS[1:]]
    expand = _head_expansion(3)
    return pl.pallas_call(
        functools.partial(_merge_ffn_kernel, final_norm=final_norm),
        grid=(b, s // tm),
        in_specs=[
            tok(d),
            pl.BlockSpec((1, 1, N_MOD, d), lambda i, j: (layer, i, 0, 0)),
            tok(MIX_WIDTH),
            *[res(dil, MIX_WIDTH) for dil in DILATIONS],
            *[res(dil, LANES) for dil in DILATIONS],
            pl.BlockSpec((None, 1, MIX_WIDTH), lambda i, j: (layer, 0, 0)),
            _resident((None, 2 * MIX_WIDTH, d), lambda i, j: (layer, 0, 0)),
            const((tm, tm)), const((tm, tm)), const(expand.shape),
            *_ffn_specs(layer, d),
            pl.BlockSpec((1, d), lambda i, j: (0, 0)),
        ],
        out_specs=tok(d),
        out_shape=jax.ShapeDtypeStruct(x.shape, F32),
        scratch_shapes=[pltpu.VMEM((len(DILATIONS), 1, tm, LANES), F32),
                        pltpu.VMEM((1, tm, LANES), F32),
                        pltpu.VMEM((tm, D_FF), BF16)],
        compiler_params=_params(2),
        name="merge_out_ffn",
    )(x, mod, ya, *outs, *lses, gain_b, w_out, *perms, expand, gain_ffn, wi, wo, final_gain)


def _pair_kv_heads(t, axis):
    n_kv = KV_A_WIDTH // HEAD_DIM
    shape = t.shape
    split = shape[:axis] + (n_kv, N_HEADS // n_kv, HEAD_DIM) + shape[axis + 1:]
    return jnp.swapaxes(t.reshape(split), axis, axis + 1).reshape(shape)


def kernel(x, c, positions, ada_w, ada_b, norm_ffn1, ffn1_wi, ffn1_wo, norm_mix, w_in, sink,
           onorm_a, onorm_b, w_out, norm_ffn2, ffn2_wi, ffn2_wo, final_norm):
    depth = ada_w.shape[0]
    w_in16 = w_in.astype(BF16)
    a_kv, b_q, b_k = MIX_WIDTH + KV_A_WIDTH, MIX_WIDTH + 2 * KV_A_WIDTH, 2 * MIX_WIDTH + 2 * KV_A_WIDTH
    w_in_b = jnp.concatenate([
        _rotary_lane_order(_pair_kv_heads(w_in16[:, :, :MIX_WIDTH], 2)),
        _rotary_lane_order(w_in16[:, :, MIX_WIDTH:a_kv]),
        w_in16[:, :, a_kv:b_q],
        _rotary_lane_order(w_in16[:, :, b_q:b_k + MIX_WIDTH]),
        w_in16[:, :, b_k + MIX_WIDTH:]], axis=2)
    w_out_b = jnp.concatenate([_pair_kv_heads(w_out[:, :MIX_WIDTH].astype(BF16), 1),
                               w_out[:, MIX_WIDTH:].astype(BF16)], axis=1)
    gain_a = _pair_kv_heads(onorm_a, 1).reshape(depth, 1, MIX_WIDTH)
    gain_b = onorm_b.reshape(depth, 1, MIX_WIDTH)
    wi1, wo1 = ffn1_wi.astype(BF16), ffn1_wo.astype(BF16)
    wi2, wo2 = ffn2_wi.astype(BF16), ffn2_wo.astype(BF16)
    g_ffn1 = norm_ffn1.reshape(depth, 1, D_MODEL)
    g_mix = norm_mix.reshape(depth, 1, D_MODEL)
    g_ffn2 = norm_ffn2.reshape(depth, 1, D_MODEL)
    g_final = final_norm.reshape(1, D_MODEL)

    mod = _modulation(c, ada_w, ada_b)
    cos, sin = _rope_tables(positions)
    for l in range(depth):
        x, qa, ka, va, qb, kb, vb = _ffn_project(x, mod, l, g_ffn1, wi1, wo1, g_mix, cos, sin,
                                                 w_in_b)
        ya = _attention_a(qa, ka, va, sink, gain_a, l)
        outs, lses = zip(*[_attention_b_branch(qb[i], kb[i], vb[i], w)
                           for i, (w, _) in enumerate(B_BRANCHES)])
        x = _merge_ffn(x, mod, l, ya, outs, lses, gain_b, w_out_b, g_ffn2, wi2, wo2, g_final,
                       l == depth - 1)
    return x
```

```python
import functools

import numpy as np
import jax
import jax.numpy as jnp
from jax import lax
from jax.experimental import pallas as pl
from jax.experimental.pallas import tpu as pltpu

D_MODEL = 1024
HEAD_DIM = 64
N_HEADS = 8
MIX_WIDTH = N_HEADS * HEAD_DIM
KV_A_WIDTH = 2 * HEAD_DIM
A_HALF_WINDOW = 128
B_BRANCHES = ((128, 1), (512, 4), (2048, 16))
DILATIONS = tuple(d for _, d in B_BRANCHES)
ROPE_THETA = 500000.0
ROPE_DIM = 16
D_FF = 2816
N_MOD = 9
D_IN = 2304
EPS = 1e-6
NEG_INF = -1e30
LOG2_E = 1.4426950408889634

LANES = 128
Q_ROWS = 128
STAGE_SKEW = 2
B_STEP_ROWS = 2048
F_CHUNK = 256
VMEM_LIMIT = 56 * 1024 * 1024

F32 = jnp.float32
BF16 = jnp.bfloat16


def _params(n_axes):
    return pltpu.CompilerParams(
        dimension_semantics=("arbitrary",) * n_axes, vmem_limit_bytes=VMEM_LIMIT)


def _resident(block_shape, index_map):
    return pl.BlockSpec(block_shape, index_map, pipeline_mode=pl.Buffered(1))


def _norm_mod(x, gain, shift, scale):
    y = x * lax.rsqrt(jnp.mean(x * x, axis=-1, keepdims=True) + EPS)
    return (y * gain) * (1.0 + scale) + shift


def _mod_kernel(c_ref, w_ref, b_ref, o_ref):
    ca = jax.nn.silu(c_ref[...]).astype(BF16)
    o_ref[0] = jnp.dot(ca, w_ref[0].astype(BF16), preferred_element_type=F32) + b_ref[0]


def _modulation(c, ada_w, ada_b):
    depth, d, n = ada_w.shape
    b = c.shape[0]
    tn = n // 8
    out = pl.pallas_call(
        _mod_kernel,
        grid=(depth, n // tn),
        in_specs=[
            pl.BlockSpec((b, d), lambda l, j: (0, 0)),
            pl.BlockSpec((1, d, tn), lambda l, j: (l, 0, j)),
            pl.BlockSpec((1, 1, tn), lambda l, j: (l, 0, j)),
        ],
        out_specs=pl.BlockSpec((1, b, tn), lambda l, j: (l, 0, j)),
        out_shape=jax.ShapeDtypeStruct((depth, b, n), F32),
        compiler_params=_params(2),
        name="adaln_mod",
    )(c, ada_w, ada_b.reshape(depth, 1, n))
    return out.reshape(depth, b, N_MOD, d)


def _ffn_step(x, mod_ref, mod_row, g_ref, wi_ref, wo_ref, act_ref):
    shift = mod_ref[0, 0, mod_row:mod_row + 1, :]
    scale = mod_ref[0, 0, mod_row + 1:mod_row + 2, :]
    gate_res = mod_ref[0, 0, mod_row + 2:mod_row + 3, :]
    hb = _norm_mod(x, g_ref[...], shift, scale).astype(BF16)
    for j in range(D_FF // F_CHUNK):
        lo = j * F_CHUNK
        gate = jnp.dot(hb, wi_ref[:, lo:lo + F_CHUNK], preferred_element_type=F32)
        up = jnp.dot(hb, wi_ref[:, D_FF + lo:D_FF + lo + F_CHUNK], preferred_element_type=F32)
        act_ref[:, lo:lo + F_CHUNK] = (jax.nn.silu(gate) * up).astype(BF16)
    y = jnp.dot(act_ref[...], wo_ref[...], preferred_element_type=F32)
    return x + (0.5 * gate_res) * y


def _ffn_specs(layer, d):
    return [pl.BlockSpec((None, 1, d), lambda *_: (layer, 0, 0)),
            _resident((None, d, 2 * D_FF), lambda *_: (layer, 0, 0)),
            _resident((None, D_FF, d), lambda *_: (layer, 0, 0))]


_ROT_HALF = ROPE_DIM // 2


def _rotary_lane_order(t):
    g = t.reshape(t.shape[:-1] + (t.shape[-1] // LANES, LANES))
    a, b = _ROT_HALF, HEAD_DIM
    g = jnp.concatenate([g[..., :a], g[..., b:b + a], g[..., 2 * a:b], g[..., a:2 * a],
                         g[..., b + a:]], axis=-1)
    return g.reshape(t.shape)


def _first_head_lanes(shape):
    lane = lax.broadcasted_iota(jnp.int32, shape, len(shape) - 1)
    return (lane < _ROT_HALF) | ((lane >= ROPE_DIM) & (lane < HEAD_DIM + _ROT_HALF))


def _rope_table_kernel(pos_ref, inv_ref, cos_ref, sin_ref):
    ts = pos_ref.shape[-1]
    ang = inv_ref[...] * pos_ref[0].astype(F32)
    cos_d, sin_d = jnp.cos(ang), jnp.sin(ang)
    plain = HEAD_DIM - ROPE_DIM
    ones, zeros = jnp.ones((plain, ts), F32), jnp.zeros((plain, ts), F32)
    cos_t = jnp.concatenate([cos_d, cos_d, ones, cos_d, cos_d, ones], axis=0)
    sin_t = jnp.concatenate([-sin_d, -sin_d, zeros, sin_d, sin_d, zeros], axis=0)
    cos_ref[0] = cos_t.T
    sin_ref[0] = sin_t.T


def _rope_tables(positions, ts=4096):
    b, s = positions.shape
    inv = (ROPE_THETA ** (-np.arange(0, ROPE_DIM, 2, dtype=np.float64) / ROPE_DIM)).astype(np.float32)
    out = pl.BlockSpec((1, ts, LANES), lambda i, j: (i, j, 0))
    return pl.pallas_call(
        _rope_table_kernel,
        grid=(b, s // ts),
        in_specs=[pl.BlockSpec((1, 1, ts), lambda i, j: (i, 0, j)),
                  pl.BlockSpec((_ROT_HALF, 1), lambda i, j: (0, 0))],
        out_specs=[out, out],
        out_shape=[jax.ShapeDtypeStruct((b, s, LANES), F32)] * 2,
        compiler_params=_params(2),
        name="rope_tables",
    )(positions.reshape(b, 1, s), jnp.asarray(inv.reshape(_ROT_HALF, 1)))


def _project_step(x, mod_ref, g_ref, cos_ref, sin_ref, w_ref, a_ref, b_refs, relays):
    n_dil = len(DILATIONS)
    tm = x.shape[0]
    hb = _norm_mod(x, g_ref[...], mod_ref[0, 0, 3:4, :], mod_ref[0, 0, 4:5, :]).astype(BF16)

    def rope(t):
        return t * cos_ref[0] + pltpu.roll(t, HEAD_DIM, axis=1) * sin_ref[0]

    proj = jnp.dot(hb, w_ref[...], preferred_element_type=F32)

    def lane_groups(col, width, rotary, mult):
        for g in range(width // LANES):
            tg = proj[:, col + g * LANES:col + (g + 1) * LANES]
            if rotary:
                tg = rope(tg)
            if mult is not None:
                tg = tg * mult
            yield g, tg

    scale = HEAD_DIM ** -0.5 * LOG2_E
    col = 0
    for width, rotary, mult in ((MIX_WIDTH, True, scale), (KV_A_WIDTH, True, None),
                                (KV_A_WIDTH, False, None)):
        for g, tg in lane_groups(col, width, rotary, mult):
            a_ref[0, :, col + g * LANES:col + (g + 1) * LANES] = tg.astype(BF16)
        col += width
    n_groups = MIX_WIDTH // LANES
    for t, (rotary, mult) in enumerate(((True, scale), (True, None), (False, None))):
        src, dst = relays[2 * t], relays[2 * t + 1]
        out_col = t * MIX_WIDTH
        for g, tg in lane_groups(col, MIX_WIDTH, rotary, mult):
            src[g] = tg
            b_refs[0][0, 0, :, out_col + g * LANES:out_col + (g + 1) * LANES] = tg.astype(BF16)
        for level in range(1, n_dil):
            coarse, fine = DILATIONS[level - 1], DILATIONS[level]
            ratio, n_coarse, n_fine = fine // coarse, tm // coarse, tm // fine
            for rc in range(coarse):
                for q in range(ratio):
                    for g in range(n_groups):
                        rows = src[g, pl.ds(rc * n_coarse + q, n_fine, stride=ratio), :]
                        r = rc + coarse * q
                        if level + 1 < n_dil:
                            dst[g, r * n_fine:(r + 1) * n_fine, :] = rows
                        b_refs[level][0, r, :, out_col + g * LANES:out_col + (g + 1) * LANES] = (
                            rows.astype(BF16))
            src, dst = dst, src
        col += MIX_WIDTH


def _ffn_proj_kernel(x_ref, mod_ref, g1_ref, wi_ref, wo_ref, gm_ref, cos_ref, sin_ref, w_ref,
                     xo_ref, *refs):
    n_dil = len(DILATIONS)
    x = _ffn_step(x_ref[0], mod_ref, 0, g1_ref, wi_ref, wo_ref, refs[1 + n_dil])
    xo_ref[0] = x
    _project_step(x, mod_ref, gm_ref, cos_ref, sin_ref, w_ref, refs[0], refs[1:1 + n_dil],
                  refs[2 + n_dil:])


def _ffn_project(x, mod, layer, gain_ffn, wi, wo, gain_mix, cos, sin, w_in, tm=512):
    b, s, d = x.shape
    tok = lambda w: pl.BlockSpec((1, tm, w), lambda i, j: (i, j, 0))
    res = lambda dil: pl.BlockSpec((1, dil, tm // dil, 3 * MIX_WIDTH),
                                   lambda i, j: (i, 0, j, 0))
    a_width = MIX_WIDTH + 2 * KV_A_WIDTH
    out_specs = [tok(d), tok(a_width)] + [res(dil) for dil in DILATIONS]
    out_shape = ([jax.ShapeDtypeStruct(x.shape, F32),
                  jax.ShapeDtypeStruct((b, s, a_width), BF16)]
                 + [jax.ShapeDtypeStruct((b, dil, s // dil, 3 * MIX_WIDTH), BF16)
                    for dil in DILATIONS])
    outs = pl.pallas_call(
        _ffn_proj_kernel,
        grid=(b, s // tm),
        in_specs=[
            tok(d),
            pl.BlockSpec((1, 1, N_MOD, d), lambda i, j: (layer, i, 0, 0)),
            *_ffn_specs(layer, d),
            pl.BlockSpec((None, 1, d), lambda i, j: (layer, 0, 0)),
            tok(LANES), tok(LANES),
            _resident((None, d, D_IN), lambda i, j: (layer, 0, 0)),
        ],
        out_specs=out_specs,
        out_shape=out_shape,
        scratch_shapes=[pltpu.VMEM((tm, D_FF), BF16)]
        + [pltpu.VMEM((MIX_WIDTH // LANES, tm, LANES), F32)] * 6,
        compiler_params=_params(2),
        name="ffn_in_proj",
    )(x, mod, gain_ffn, wi, wo, gain_mix, cos, sin, w_in)
    return outs[0], outs[1], outs[2:]


def _fill_band_bias(bias_ref, hw):
    n_keys = bias_ref.shape[1]
    jj = lax.broadcasted_iota(jnp.int32, (n_keys, 2 * Q_ROWS), 0)
    ii = lax.broadcasted_iota(jnp.int32, (n_keys, 2 * Q_ROWS), 1) & (Q_ROWS - 1)
    rel = jj - ii
    band = (rel >= 0) & (rel <= 2 * hw)
    for c, valid in enumerate((band & (jj >= hw), band, band & (jj < n_keys - hw))):
        bias_ref[c] = jnp.where(valid, 0.0, NEG_INF)


def _key_window(prev_ref, cur_ref, next_ref, sb, hw, cols):
    m = cur_ref.shape[0]
    lo, hi = sb * Q_ROWS - hw, (sb + 1) * Q_ROWS + hw
    parts = [cur_ref[max(lo, 0):min(hi, m), cols]]
    if lo < 0:
        parts.insert(0, prev_ref[:, cols])
    if hi > m:
        parts.append(next_ref[:, cols])
    return parts[0] if len(parts) == 1 else jnp.concatenate(parts, axis=0)


def _band_case(sb, n_sub, step_axis):
    if sb == 0:
        return jnp.where(pl.program_id(step_axis) == 0, 0, 1)
    if sb == n_sub - 1:
        return jnp.where(pl.program_id(step_axis) == pl.num_programs(step_axis) - 1, 2, 1)
    return 1


def _attend_groups(groups, emit):
    n = len(groups)
    state = [None] * n

    def scores(g):
        q_pair, k_win, v_win, bias, sink_row = groups[g]()
        is_lo = _first_head_lanes(q_pair.shape)
        zero = jnp.zeros_like(q_pair)
        q_both = jnp.concatenate(
            [jnp.where(is_lo, q_pair, zero), jnp.where(is_lo, zero, q_pair)], axis=0)
        s = lax.dot_general(k_win, q_both, (((1,), (1,)), ((), ())), preferred_element_type=F32)
        state[g] = (s + bias, v_win, sink_row)

    def softmax(g):
        s, v_win, sink_row = state[g]
        m = jnp.max(s, axis=0, keepdims=True)
        if sink_row is not None:
            m = jnp.maximum(m, sink_row)
        e = jnp.exp2(s - m)
        l_keys = jnp.sum(e, axis=0, keepdims=True)
        denom = l_keys if sink_row is None else l_keys + jnp.exp2(sink_row - m)
        state[g] = (e.astype(BF16), v_win, denom, m + jnp.log2(l_keys))

    def values(g):
        e, v_win, denom, lse = state[g]
        pv = lax.dot_general(v_win, e, (((0,), (0,)), ((), ())), preferred_element_type=F32)
        inv = 1.0 / denom
        out_t = jnp.concatenate(
            [pv[0:HEAD_DIM, 0:Q_ROWS] * inv[:, 0:Q_ROWS],
             pv[HEAD_DIM:2 * HEAD_DIM, Q_ROWS:2 * Q_ROWS] * inv[:, Q_ROWS:]], axis=0)
        state[g] = None
        emit(g, out_t, lse)

    for t in range(n + 2 * STAGE_SKEW):
        if t < n:
            scores(t)
        if STAGE_SKEW <= t < n + STAGE_SKEW:
            softmax(t - STAGE_SKEW)
        if t >= 2 * STAGE_SKEW:
            values(t - 2 * STAGE_SKEW)


def _attn_a_kernel(sink_ref, q_ref, kp_ref, kc_ref, kn_ref, vp_ref, vc_ref, vn_ref, g_ref,
                   y_ref, bias_ref, *, layer):
    hw = A_HALF_WINDOW
    m = q_ref.shape[0]
    n_sub = m // Q_ROWS
    n_pairs = MIX_WIDTH // LANES

    @pl.when((pl.program_id(0) == 0) & (pl.program_id(1) == 0))
    def _():
        _fill_band_bias(bias_ref, hw)

    is_lo_query = lax.broadcasted_iota(jnp.int32, (1, 2 * Q_ROWS), 1) < Q_ROWS
    all_cols = slice(0, KV_A_WIDTH)

    def group(sb, p):
        def load():
            sink_row = jnp.where(is_lo_query, sink_ref[layer, p],
                                 sink_ref[layer, p + n_pairs]) * LOG2_E
            return (q_ref[sb * Q_ROWS:(sb + 1) * Q_ROWS, p * LANES:(p + 1) * LANES],
                    _key_window(kp_ref, kc_ref, kn_ref, sb, hw, all_cols),
                    _key_window(vp_ref, vc_ref, vn_ref, sb, hw, all_cols),
                    bias_ref[_band_case(sb, n_sub, 1)], sink_row)
        return load

    pending = []

    def emit(g, out_t, _):
        pending.append(out_t)
        if len(pending) < n_pairs:
            return
        sb = g // n_pairs
        ssq = sum(jnp.sum(o * o, axis=0, keepdims=True) for o in pending)
        inv = lax.rsqrt(ssq / MIX_WIDTH + EPS)
        for p, o in enumerate(pending):
            y = (o * inv).T * g_ref[:, p * LANES:(p + 1) * LANES]
            y_ref[sb * Q_ROWS:(sb + 1) * Q_ROWS, p * LANES:(p + 1) * LANES] = y.astype(BF16)
        pending.clear()

    _attend_groups([group(sb, p) for sb in range(n_sub) for p in range(n_pairs)], emit)


def _attention_a(qkv, sink, gain, layer, m=2048):
    b, s, _ = qkv.shape
    hw = A_HALF_WINDOW
    per = m // hw
    last = s // hw - 1
    k_col, v_col = MIX_WIDTH // KV_A_WIDTH, MIX_WIDTH // KV_A_WIDTH + 1
    cur = lambda w, c: pl.BlockSpec((None, m, w), lambda i, j: (i, j, c))
    prev = lambda c: pl.BlockSpec((None, hw, KV_A_WIDTH),
                                  lambda i, j: (i, jnp.maximum(j * per - 1, 0), c))
    nxt = lambda c: pl.BlockSpec((None, hw, KV_A_WIDTH),
                                 lambda i, j: (i, jnp.minimum((j + 1) * per, last), c))
    kern = functools.partial(_attn_a_kernel, layer=layer)
    return pl.pallas_call(
        kern,
        grid=(b, s // m),
        in_specs=[
            pl.BlockSpec(memory_space=pltpu.SMEM),
            cur(MIX_WIDTH, 0),
            prev(k_col), cur(KV_A_WIDTH, k_col), nxt(k_col),
            prev(v_col), cur(KV_A_WIDTH, v_col), nxt(v_col),
            pl.BlockSpec((None, 1, MIX_WIDTH), lambda i, j: (layer, 0, 0)),
        ],
        out_specs=cur(MIX_WIDTH, 0),
        out_shape=jax.ShapeDtypeStruct((b, s, MIX_WIDTH), BF16),
        scratch_shapes=[pltpu.VMEM((3, Q_ROWS + 2 * hw, 2 * Q_ROWS), F32)],
        compiler_params=_params(2),
        name="attn_a",
    )(sink, *[qkv] * 7, gain)


def _attn_b_kernel(q_ref, kp_ref, kc_ref, kn_ref, vp_ref, vc_ref, vn_ref,
                   o_ref, lse_ref, bias_ref, *, hw):
    n_res, m = q_ref.shape[0], q_ref.shape[1]
    n_sub = m // Q_ROWS
    n_pairs = MIX_WIDTH // LANES
    rows_per_head = LANES // N_HEADS

    @pl.when((pl.program_id(0) == 0) & (pl.program_id(1) == 0) & (pl.program_id(2) == 0))
    def _():
        _fill_band_bias(bias_ref, hw)

    def group(r, sb, p):
        cols = slice(p * LANES, (p + 1) * LANES)
        return lambda: (q_ref[r, sb * Q_ROWS:(sb + 1) * Q_ROWS, cols],
                        _key_window(kp_ref.at[r], kc_ref.at[r], kn_ref.at[r], sb, hw, cols),
                        _key_window(vp_ref.at[r], vc_ref.at[r], vn_ref.at[r], sb, hw, cols),
                        bias_ref[_band_case(sb, n_sub, 2)], None)

    lse_rows = []

    def emit(g, out_t, lse):
        r, sb, p = g // (n_sub * n_pairs), (g // n_pairs) % n_sub, g % n_pairs
        rows = slice(sb * Q_ROWS, (sb + 1) * Q_ROWS)
        o_ref[r, rows, p * LANES:(p + 1) * LANES] = out_t.T.astype(BF16)
        lse_rows.extend([lse[:, 0:Q_ROWS], lse[:, Q_ROWS:2 * Q_ROWS]])
        if len(lse_rows) == N_HEADS:
            lse_t = jnp.concatenate(
                [jnp.broadcast_to(v, (rows_per_head, Q_ROWS)) for v in lse_rows], axis=0)
            lse_ref[r, rows, :] = lse_t.T
            lse_rows.clear()

    _attend_groups([group(r, sb, p) for r in range(n_res) for sb in range(n_sub)
                    for p in range(n_pairs)], emit)


def _attention_b_branch(qkv, window):
    b, dilation, seq, _ = qkv.shape
    w = MIX_WIDTH
    hw = window // (2 * dilation)
    m = min(seq, B_STEP_ROWS)
    n_res = min(dilation, B_STEP_ROWS // m)
    per = m // hw
    last = seq // hw - 1
    cur = lambda width, c: pl.BlockSpec((None, n_res, m, width), lambda i, r, j: (i, r, j, c))
    prev = lambda c: pl.BlockSpec((None, n_res, hw, w),
                                  lambda i, r, j: (i, r, jnp.maximum(j * per - 1, 0), c))
    nxt = lambda c: pl.BlockSpec((None, n_res, hw, w),
                                 lambda i, r, j: (i, r, jnp.minimum((j + 1) * per, last), c))
    kern = functools.partial(_attn_b_kernel, hw=hw)
    return pl.pallas_call(
        kern,
        grid=(b, dilation // n_res, seq // m),
        in_specs=[cur(w, 0), prev(1), cur(w, 1), nxt(1), prev(2), cur(w, 2), nxt(2)],
        out_specs=[cur(w, 0), cur(LANES, 0)],
        out_shape=[jax.ShapeDtypeStruct((b, dilation, seq, w), BF16),
                   jax.ShapeDtypeStruct((b, dilation, seq, LANES), F32)],
        scratch_shapes=[pltpu.VMEM((3, Q_ROWS + 2 * hw, 2 * Q_ROWS), F32)],
        compiler_params=_params(3),
        name=f"attn_b_d{dilation}",
    )(*[qkv] * 7)


def _split_bf16(v):
    hi = v.astype(BF16)
    rest = v - hi.astype(F32)
    mid = rest.astype(BF16)
    return hi, mid, (rest - mid.astype(F32)).astype(BF16)


def _merge_ffn_kernel(x_ref, mod_ref, ya_ref, o1_ref, o2_ref, o3_ref, l1_ref, l2_ref, l3_ref,
                      gb_ref, w_ref, perm2_ref, perm3_ref, expand_ref, g2_ref, wi_ref, wo_ref,
                      fg_ref, o_ref, relay_l, tmp_l, act_ref, *, final_norm):
    x = x_ref[0]
    tm = x.shape[0]

    def natural_order(src_ref, dil, dst, tmp, n_lane_groups):
        level = DILATIONS.index(dil)
        bufs = (tmp, dst)
        for lv in range(level, 0, -1):
            coarse, fine = DILATIONS[lv - 1], DILATIONS[lv]
            ratio, n_coarse, n_fine = fine // coarse, tm // coarse, tm // fine
            out = bufs[lv % 2]
            for rc in range(coarse):
                for q in range(ratio):
                    r = rc + coarse * q
                    for g in range(n_lane_groups):
                        if lv == level:
                            rows = src_ref[r, :, g * LANES:(g + 1) * LANES]
                        else:
                            rows = bufs[(lv + 1) % 2][g, r * n_fine:(r + 1) * n_fine, :]
                        out[g, pl.ds(rc * n_coarse + q, n_fine, stride=ratio), :] = rows

    lses = []
    for i, (l_ref, dil) in enumerate(zip((l1_ref, l2_ref, l3_ref), DILATIONS)):
        if dil == 1:
            lses.append(l_ref[0])
        else:
            natural_order(l_ref, dil, relay_l.at[i], tmp_l, 1)
            lses.append(relay_l[i, 0])
    mx = jnp.maximum(jnp.maximum(lses[0], lses[1]), lses[2])
    es = [jnp.exp2(l - mx) for l in lses]
    den = es[0] + es[1] + es[2]

    def head_weight(e):
        terms = jnp.concatenate(_split_bf16(e / den), axis=1)
        return jnp.dot(terms, expand_ref[...], preferred_element_type=F32)

    o1 = o1_ref[0].astype(F32)
    o2 = jnp.dot(perm2_ref[...], o2_ref[...].reshape(tm, MIX_WIDTH), preferred_element_type=F32)
    o3 = jnp.dot(perm3_ref[...], o3_ref[...].reshape(tm, MIX_WIDTH), preferred_element_type=F32)
    ob = o1 + head_weight(es[1]) * (o2 - o1) + head_weight(es[2]) * (o3 - o1)
    inv = lax.rsqrt(jnp.mean(ob * ob, axis=1, keepdims=True) + EPS)
    yb = ((ob * inv) * gb_ref[...]).astype(BF16)
    mix = jnp.dot(ya_ref[0], w_ref[0:MIX_WIDTH, :], preferred_element_type=F32)
    mix = mix + jnp.dot(yb, w_ref[MIX_WIDTH:2 * MIX_WIDTH, :], preferred_element_type=F32)
    x = x + mod_ref[0, 0, 5:6, :] * mix
    out = _ffn_step(x, mod_ref, 6, g2_ref, wi_ref, wo_ref, act_ref)
    if final_norm:
        out = out * lax.rsqrt(jnp.mean(out * out, axis=-1, keepdims=True) + EPS) * fg_ref[...]
    o_ref[0] = out


def _residue_permutation(tm, dil):
    token = np.arange(tm)
    perm = np.zeros((tm, tm), np.float32)
    perm[token, (token % dil) * (tm // dil) + token // dil] = 1.0
    return jnp.asarray(perm, BF16)


def _head_expansion(n_terms):
    expand = np.zeros((n_terms, LANES, MIX_WIDTH), np.float32)
    for h in range(N_HEADS):
        expand[:, h * (LANES // N_HEADS), h * HEAD_DIM:(h + 1) * HEAD_DIM] = 1.0
    return jnp.asarray(expand.reshape(n_terms * LANES, MIX_WIDTH), BF16)


def _merge_ffn(x, mod, layer, ya, outs, lses, gain_b, w_out, gain_ffn, wi, wo, final_gain,
               final_norm, tm=512):
    b, s, d = x.shape
    tok = lambda w: pl.BlockSpec((1, tm, w), lambda i, j: (i, j, 0))
    res = lambda dil, w: pl.BlockSpec((None, dil, tm // dil, w), lambda i, j: (i, 0, j, 0))
    const = lambda shape: _resident(shape, lambda i, j: (0, 0))
    perms = [_residue_permutation(tm, dil) for dil in DILATIONS[1:]]
    expand = _head_expansion(3)
    return pl.pallas_call(
        functools.partial(_merge_ffn_kernel, final_norm=final_norm),
        grid=(b, s // tm),
        in_specs=[
            tok(d),
            pl.BlockSpec((1, 1, N_MOD, d), lambda i, j: (layer, i, 0, 0)),
            tok(MIX_WIDTH),
            *[res(dil, MIX_WIDTH) for dil in DILATIONS],
            *[res(dil, LANES) for dil in DILATIONS],
            pl.BlockSpec((None, 1, MIX_WIDTH), lambda i, j: (layer, 0, 0)),
            _resident((None, 2 * MIX_WIDTH, d), lambda i, j: (layer, 0, 0)),
            const((tm, tm)), const((tm, tm)), const(expand.shape),
            *_ffn_specs(layer, d),
            pl.BlockSpec((1, d), lambda i, j: (0, 0)),
        ],
        out_specs=tok(d),
        out_shape=jax.ShapeDtypeStruct(x.shape, F32),
        scratch_shapes=[pltpu.VMEM((len(DILATIONS), 1, tm, LANES), F32),
                        pltpu.VMEM((1, tm, LANES), F32),
                        pltpu.VMEM((tm, D_FF), BF16)],
        compiler_params=_params(2),
        name="merge_out_ffn",
    )(x, mod, ya, *outs, *lses, gain_b, w_out, *perms, expand, gain_ffn, wi, wo, final_gain)


def _pair_kv_heads(t, axis):
    n_kv = KV_A_WIDTH // HEAD_DIM
    shape = t.shape
    split = shape[:axis] + (n_kv, N_HEADS // n_kv, HEAD_DIM) + shape[axis + 1:]
    return jnp.swapaxes(t.reshape(split), axis, axis + 1).reshape(shape)


def kernel(x, c, positions, ada_w, ada_b, norm_ffn1, ffn1_wi, ffn1_wo, norm_mix, w_in, sink,
           onorm_a, onorm_b, w_out, norm_ffn2, ffn2_wi, ffn2_wo, final_norm):
    depth = ada_w.shape[0]
    w_in16 = w_in.astype(BF16)
    a_kv, b_q, b_k = MIX_WIDTH + KV_A_WIDTH, MIX_WIDTH + 2 * KV_A_WIDTH, 2 * MIX_WIDTH + 2 * KV_A_WIDTH
    w_in_b = jnp.concatenate([
        _rotary_lane_order(_pair_kv_heads(w_in16[:, :, :MIX_WIDTH], 2)),
        _rotary_lane_order(w_in16[:, :, MIX_WIDTH:a_kv]),
        w_in16[:, :, a_kv:b_q],
        _rotary_lane_order(w_in16[:, :, b_q:b_k + MIX_WIDTH]),
        w_in16[:, :, b_k + MIX_WIDTH:]], axis=2)
    w_out_b = jnp.concatenate([_pair_kv_heads(w_out[:, :MIX_WIDTH].astype(BF16), 1),
                               w_out[:, MIX_WIDTH:].astype(BF16)], axis=1)
    gain_a = _pair_kv_heads(onorm_a, 1).reshape(depth, 1, MIX_WIDTH)
    gain_b = onorm_b.reshape(depth, 1, MIX_WIDTH)
    wi1, wo1 = ffn1_wi.astype(BF16), ffn1_wo.astype(BF16)
    wi2, wo2 = ffn2_wi.astype(BF16), ffn2_wo.astype(BF16)
    g_ffn1 = norm_ffn1.reshape(depth, 1, D_MODEL)
    g_mix = norm_mix.reshape(depth, 1, D_MODEL)
    g_ffn2 = norm_ffn2.reshape(depth, 1, D_MODEL)
    g_final = final_norm.reshape(1, D_MODEL)

    mod = _modulation(c, ada_w, ada_b)
    cos, sin = _rope_tables(positions)
    for l in range(depth):
        x, qkv_a, qkv_b = _ffn_project(x, mod, l, g_ffn1, wi1, wo1, g_mix, cos, sin, w_in_b)
        ya = _attention_a(qkv_a, sink, gain_a, l)
        outs, lses = zip(*[_attention_b_branch(qkv, w)
                           for qkv, (w, _) in zip(qkv_b, B_BRANCHES)])
        x = _merge_ffn(x, mod, l, ya, outs, lses, gain_b, w_out_b, g_ffn2, wi2, wo2, g_final,
                       l == depth - 1)
    return x
```

```python
import functools

import numpy as np
import jax
import jax.numpy as jnp
from jax import lax
from jax.experimental import pallas as pl
from jax.experimental.pallas import tpu as pltpu

D_MODEL = 1024
HEAD_DIM = 64
N_HEADS = 8
MIX_WIDTH = N_HEADS * HEAD_DIM
KV_A_WIDTH = 2 * HEAD_DIM
A_HALF_WINDOW = 128
B_BRANCHES = ((128, 1), (512, 4), (2048, 16))
DILATIONS = tuple(d for _, d in B_BRANCHES)
ROPE_THETA = 500000.0
ROPE_DIM = 16
D_FF = 2816
N_MOD = 9
D_IN = 2304
EPS = 1e-6
NEG_INF = -1e30
LOG2_E = 1.4426950408889634

LANES = 128
Q_ROWS = 128
STAGE_SKEW = 2
B_STEP_ROWS = 2048
F_CHUNK = 256
VMEM_LIMIT = 56 * 1024 * 1024

F32 = jnp.float32
BF16 = jnp.bfloat16


def _params(n_axes):
    return pltpu.CompilerParams(
        dimension_semantics=("arbitrary",) * n_axes, vmem_limit_bytes=VMEM_LIMIT)


def _resident(block_shape, index_map):
    return pl.BlockSpec(block_shape, index_map, pipeline_mode=pl.Buffered(1))


def _norm_mod(x, gain, shift, scale):
    y = x * lax.rsqrt(jnp.mean(x * x, axis=-1, keepdims=True) + EPS)
    return (y * gain) * (1.0 + scale) + shift


def _mod_kernel(c_ref, w_ref, b_ref, o_ref):
    ca = jax.nn.silu(c_ref[...]).astype(BF16)
    o_ref[0] = jnp.dot(ca, w_ref[0].astype(BF16), preferred_element_type=F32) + b_ref[0]


def _modulation(c, ada_w, ada_b):
    depth, d, n = ada_w.shape
    b = c.shape[0]
    tn = n // 8
    out = pl.pallas_call(
        _mod_kernel,
        grid=(depth, n // tn),
        in_specs=[
            pl.BlockSpec((b, d), lambda l, j: (0, 0)),
            pl.BlockSpec((1, d, tn), lambda l, j: (l, 0, j)),
            pl.BlockSpec((1, 1, tn), lambda l, j: (l, 0, j)),
        ],
        out_specs=pl.BlockSpec((1, b, tn), lambda l, j: (l, 0, j)),
        out_shape=jax.ShapeDtypeStruct((depth, b, n), F32),
        compiler_params=_params(2),
        name="adaln_mod",
    )(c, ada_w, ada_b.reshape(depth, 1, n))
    return out.reshape(depth, b, N_MOD, d)


def _ffn_step(x, mod_ref, mod_row, g_ref, wi_ref, wo_ref, act_ref):
    shift = mod_ref[0, 0, mod_row:mod_row + 1, :]
    scale = mod_ref[0, 0, mod_row + 1:mod_row + 2, :]
    gate_res = mod_ref[0, 0, mod_row + 2:mod_row + 3, :]
    hb = _norm_mod(x, g_ref[...], shift, scale).astype(BF16)
    for j in range(D_FF // F_CHUNK):
        lo = j * F_CHUNK
        gate = jnp.dot(hb, wi_ref[:, lo:lo + F_CHUNK], preferred_element_type=F32)
        up = jnp.dot(hb, wi_ref[:, D_FF + lo:D_FF + lo + F_CHUNK], preferred_element_type=F32)
        act_ref[:, lo:lo + F_CHUNK] = (jax.nn.silu(gate) * up).astype(BF16)
    y = jnp.dot(act_ref[...], wo_ref[...], preferred_element_type=F32)
    return x + (0.5 * gate_res) * y


def _ffn_specs(layer, d):
    return [pl.BlockSpec((None, 1, d), lambda *_: (layer, 0, 0)),
            _resident((None, d, 2 * D_FF), lambda *_: (layer, 0, 0)),
            _resident((None, D_FF, d), lambda *_: (layer, 0, 0))]


_ROT_HALF = ROPE_DIM // 2


def _rotary_lane_order(t):
    g = t.reshape(t.shape[:-1] + (t.shape[-1] // LANES, LANES))
    a, b = _ROT_HALF, HEAD_DIM
    g = jnp.concatenate([g[..., :a], g[..., b:b + a], g[..., 2 * a:b], g[..., a:2 * a],
                         g[..., b + a:]], axis=-1)
    return g.reshape(t.shape)


def _first_head_lanes(shape):
    lane = lax.broadcasted_iota(jnp.int32, shape, len(shape) - 1)
    return (lane < _ROT_HALF) | ((lane >= ROPE_DIM) & (lane < HEAD_DIM + _ROT_HALF))


def _rope_table_kernel(pos_ref, inv_ref, cos_ref, sin_ref):
    ts = pos_ref.shape[-1]
    ang = inv_ref[...] * pos_ref[0].astype(F32)
    cos_d, sin_d = jnp.cos(ang), jnp.sin(ang)
    plain = HEAD_DIM - ROPE_DIM
    ones, zeros = jnp.ones((plain, ts), F32), jnp.zeros((plain, ts), F32)
    cos_t = jnp.concatenate([cos_d, cos_d, ones, cos_d, cos_d, ones], axis=0)
    sin_t = jnp.concatenate([-sin_d, -sin_d, zeros, sin_d, sin_d, zeros], axis=0)
    cos_ref[0] = cos_t.T
    sin_ref[0] = sin_t.T


def _rope_tables(positions, ts=4096):
    b, s = positions.shape
    inv = (ROPE_THETA ** (-np.arange(0, ROPE_DIM, 2, dtype=np.float64) / ROPE_DIM)).astype(np.float32)
    out = pl.BlockSpec((1, ts, LANES), lambda i, j: (i, j, 0))
    return pl.pallas_call(
        _rope_table_kernel,
        grid=(b, s // ts),
        in_specs=[pl.BlockSpec((1, 1, ts), lambda i, j: (i, 0, j)),
                  pl.BlockSpec((_ROT_HALF, 1), lambda i, j: (0, 0))],
        out_specs=[out, out],
        out_shape=[jax.ShapeDtypeStruct((b, s, LANES), F32)] * 2,
        compiler_params=_params(2),
        name="rope_tables",
    )(positions.reshape(b, 1, s), jnp.asarray(inv.reshape(_ROT_HALF, 1)))


def _project_step(x, mod_ref, g_ref, cos_ref, sin_ref, w_ref, a_ref, b_refs, relays):
    n_dil = len(DILATIONS)
    tm = x.shape[0]
    hb = _norm_mod(x, g_ref[...], mod_ref[0, 0, 3:4, :], mod_ref[0, 0, 4:5, :]).astype(BF16)

    def rope(t):
        return t * cos_ref[0] + pltpu.roll(t, HEAD_DIM, axis=1) * sin_ref[0]

    proj = jnp.dot(hb, w_ref[...], preferred_element_type=F32)

    def lane_groups(col, width, rotary, mult):
        for g in range(width // LANES):
            tg = proj[:, col + g * LANES:col + (g + 1) * LANES]
            if rotary:
                tg = rope(tg)
            if mult is not None:
                tg = tg * mult
            yield g, tg

    scale = HEAD_DIM ** -0.5 * LOG2_E
    col = 0
    for width, rotary, mult in ((MIX_WIDTH, True, scale), (KV_A_WIDTH, True, None),
                                (KV_A_WIDTH, False, None)):
        for g, tg in lane_groups(col, width, rotary, mult):
            a_ref[0, :, col + g * LANES:col + (g + 1) * LANES] = tg.astype(BF16)
        col += width
    n_groups = MIX_WIDTH // LANES
    for t, (rotary, mult) in enumerate(((True, scale), (True, None), (False, None))):
        src, dst = relays[2 * t], relays[2 * t + 1]
        out_col = t * MIX_WIDTH
        for g, tg in lane_groups(col, MIX_WIDTH, rotary, mult):
            src[g] = tg
            b_refs[0][0, 0, :, out_col + g * LANES:out_col + (g + 1) * LANES] = tg.astype(BF16)
        for level in range(1, n_dil):
            coarse, fine = DILATIONS[level - 1], DILATIONS[level]
            ratio, n_coarse, n_fine = fine // coarse, tm // coarse, tm // fine
            for rc in range(coarse):
                for q in range(ratio):
                    for g in range(n_groups):
                        rows = src[g, pl.ds(rc * n_coarse + q, n_fine, stride=ratio), :]
                        r = rc + coarse * q
                        if level + 1 < n_dil:
                            dst[g, r * n_fine:(r + 1) * n_fine, :] = rows
                        b_refs[level][0, r, :, out_col + g * LANES:out_col + (g + 1) * LANES] = (
                            rows.astype(BF16))
            src, dst = dst, src
        col += MIX_WIDTH


def _ffn_proj_kernel(x_ref, mod_ref, g1_ref, wi_ref, wo_ref, gm_ref, cos_ref, sin_ref, w_ref,
                     xo_ref, *refs):
    n_dil = len(DILATIONS)
    x = _ffn_step(x_ref[0], mod_ref, 0, g1_ref, wi_ref, wo_ref, refs[1 + n_dil])
    xo_ref[0] = x
    _project_step(x, mod_ref, gm_ref, cos_ref, sin_ref, w_ref, refs[0], refs[1:1 + n_dil],
                  refs[2 + n_dil:])


def _ffn_project(x, mod, layer, gain_ffn, wi, wo, gain_mix, cos, sin, w_in, tm=512):
    b, s, d = x.shape
    tok = lambda w: pl.BlockSpec((1, tm, w), lambda i, j: (i, j, 0))
    res = lambda dil: pl.BlockSpec((1, dil, tm // dil, 3 * MIX_WIDTH),
                                   lambda i, j: (i, 0, j, 0))
    a_width = MIX_WIDTH + 2 * KV_A_WIDTH
    out_specs = [tok(d), tok(a_width)] + [res(dil) for dil in DILATIONS]
    out_shape = ([jax.ShapeDtypeStruct(x.shape, F32),
                  jax.ShapeDtypeStruct((b, s, a_width), BF16)]
                 + [jax.ShapeDtypeStruct((b, dil, s // dil, 3 * MIX_WIDTH), BF16)
                    for dil in DILATIONS])
    outs = pl.pallas_call(
        _ffn_proj_kernel,
        grid=(b, s // tm),
        in_specs=[
            tok(d),
            pl.BlockSpec((1, 1, N_MOD, d), lambda i, j: (layer, i, 0, 0)),
            *_ffn_specs(layer, d),
            pl.BlockSpec((None, 1, d), lambda i, j: (layer, 0, 0)),
            tok(LANES), tok(LANES),
            _resident((None, d, D_IN), lambda i, j: (layer, 0, 0)),
        ],
        out_specs=out_specs,
        out_shape=out_shape,
        scratch_shapes=[pltpu.VMEM((tm, D_FF), BF16)]
        + [pltpu.VMEM((MIX_WIDTH // LANES, tm, LANES), F32)] * 6,
        compiler_params=_params(2),
        name="ffn_in_proj",
    )(x, mod, gain_ffn, wi, wo, gain_mix, cos, sin, w_in)
    return outs[0], outs[1], outs[2:]


def _band_penalty(hw):
    n_keys = Q_ROWS + 2 * hw
    rel = np.arange(n_keys)[:, None] - np.arange(Q_ROWS)[None, :]
    band = (rel >= 0) & (rel <= 2 * hw)
    key = np.arange(n_keys)[:, None]
    valid = np.stack([band & (key >= hw), band, band & (key < n_keys - hw)])
    return jnp.asarray(np.where(valid, 0.0, NEG_INF), BF16)


def _key_window(prev_ref, cur_ref, next_ref, sb, hw, cols):
    m = cur_ref.shape[0]
    lo, hi = sb * Q_ROWS - hw, (sb + 1) * Q_ROWS + hw
    parts = [cur_ref[max(lo, 0):min(hi, m), cols]]
    if lo < 0:
        parts.insert(0, prev_ref[:, cols])
    if hi > m:
        parts.append(next_ref[:, cols])
    return parts[0] if len(parts) == 1 else jnp.concatenate(parts, axis=0)


def _band_case(sb, n_sub, step_axis):
    if sb == 0:
        return jnp.where(pl.program_id(step_axis) == 0, 0, 1)
    if sb == n_sub - 1:
        return jnp.where(pl.program_id(step_axis) == pl.num_programs(step_axis) - 1, 2, 1)
    return 1


def _attend_groups(groups, emit):
    n = len(groups)
    state = [None] * n
    row = lax.broadcasted_iota(jnp.int32, (2 * Q_ROWS, Q_ROWS), 0) & (Q_ROWS - 1)
    one_hot = (row == lax.broadcasted_iota(jnp.int32, (2 * Q_ROWS, Q_ROWS), 1)).astype(BF16)

    def scores(g):
        q_pair, k_win, v_win, penalty, sink_row = groups[g]()
        is_lo = _first_head_lanes(q_pair.shape)
        zero = jnp.zeros_like(q_pair)
        q_both = jnp.concatenate(
            [jnp.where(is_lo, q_pair, zero), jnp.where(is_lo, zero, q_pair)], axis=0)
        s = lax.dot_general(jnp.concatenate([k_win, penalty], axis=1),
                            jnp.concatenate([q_both, one_hot], axis=1),
                            (((1,), (1,)), ((), ())), preferred_element_type=F32)
        state[g] = (s, v_win, sink_row)

    def softmax(g):
        s, v_win, sink_row = state[g]
        m = jnp.max(s, axis=0, keepdims=True)
        if sink_row is not None:
            m = jnp.maximum(m, sink_row)
        e = jnp.exp2(s - m)
        l_keys = jnp.sum(e, axis=0, keepdims=True)
        denom = l_keys if sink_row is None else l_keys + jnp.exp2(sink_row - m)
        state[g] = (e.astype(BF16), v_win, denom, m + jnp.log2(l_keys))

    def values(g):
        e, v_win, denom, lse = state[g]
        pv = lax.dot_general(v_win, e, (((0,), (0,)), ((), ())), preferred_element_type=F32)
        inv = 1.0 / denom
        out_t = jnp.concatenate(
            [pv[0:HEAD_DIM, 0:Q_ROWS] * inv[:, 0:Q_ROWS],
             pv[HEAD_DIM:2 * HEAD_DIM, Q_ROWS:2 * Q_ROWS] * inv[:, Q_ROWS:]], axis=0)
        state[g] = None
        emit(g, out_t, lse)

    for t in range(n + 2 * STAGE_SKEW):
        if t < n:
            scores(t)
        if STAGE_SKEW <= t < n + STAGE_SKEW:
            softmax(t - STAGE_SKEW)
        if t >= 2 * STAGE_SKEW:
            values(t - 2 * STAGE_SKEW)


def _attn_a_kernel(sink_ref, q_ref, kp_ref, kc_ref, kn_ref, vp_ref, vc_ref, vn_ref, g_ref,
                   pen_ref, y_ref, *, layer):
    hw = A_HALF_WINDOW
    m = q_ref.shape[0]
    n_sub = m // Q_ROWS
    n_pairs = MIX_WIDTH // LANES
    is_lo_query = lax.broadcasted_iota(jnp.int32, (1, 2 * Q_ROWS), 1) < Q_ROWS
    all_cols = slice(0, KV_A_WIDTH)

    def group(sb, p):
        def load():
            sink_row = jnp.where(is_lo_query, sink_ref[layer, p],
                                 sink_ref[layer, p + n_pairs]) * LOG2_E
            return (q_ref[sb * Q_ROWS:(sb + 1) * Q_ROWS, p * LANES:(p + 1) * LANES],
                    _key_window(kp_ref, kc_ref, kn_ref, sb, hw, all_cols),
                    _key_window(vp_ref, vc_ref, vn_ref, sb, hw, all_cols),
                    pen_ref[_band_case(sb, n_sub, 1)], sink_row)
        return load

    pending = []

    def emit(g, out_t, _):
        pending.append(out_t)
        if len(pending) < n_pairs:
            return
        sb = g // n_pairs
        ssq = sum(jnp.sum(o * o, axis=0, keepdims=True) for o in pending)
        inv = lax.rsqrt(ssq / MIX_WIDTH + EPS)
        for p, o in enumerate(pending):
            y = (o * inv).T * g_ref[:, p * LANES:(p + 1) * LANES]
            y_ref[sb * Q_ROWS:(sb + 1) * Q_ROWS, p * LANES:(p + 1) * LANES] = y.astype(BF16)
        pending.clear()

    _attend_groups([group(sb, p) for sb in range(n_sub) for p in range(n_pairs)], emit)


def _attention_a(qkv, sink, gain, layer, m=2048):
    b, s, _ = qkv.shape
    hw = A_HALF_WINDOW
    per = m // hw
    last = s // hw - 1
    k_col, v_col = MIX_WIDTH // KV_A_WIDTH, MIX_WIDTH // KV_A_WIDTH + 1
    cur = lambda w, c: pl.BlockSpec((None, m, w), lambda i, j: (i, j, c))
    prev = lambda c: pl.BlockSpec((None, hw, KV_A_WIDTH),
                                  lambda i, j: (i, jnp.maximum(j * per - 1, 0), c))
    nxt = lambda c: pl.BlockSpec((None, hw, KV_A_WIDTH),
                                 lambda i, j: (i, jnp.minimum((j + 1) * per, last), c))
    kern = functools.partial(_attn_a_kernel, layer=layer)
    return pl.pallas_call(
        kern,
        grid=(b, s // m),
        in_specs=[
            pl.BlockSpec(memory_space=pltpu.SMEM),
            cur(MIX_WIDTH, 0),
            prev(k_col), cur(KV_A_WIDTH, k_col), nxt(k_col),
            prev(v_col), cur(KV_A_WIDTH, v_col), nxt(v_col),
            pl.BlockSpec((None, 1, MIX_WIDTH), lambda i, j: (layer, 0, 0)),
            _resident((3, Q_ROWS + 2 * hw, Q_ROWS), lambda i, j: (0, 0, 0)),
        ],
        out_specs=cur(MIX_WIDTH, 0),
        out_shape=jax.ShapeDtypeStruct((b, s, MIX_WIDTH), BF16),
        compiler_params=_params(2),
        name="attn_a",
    )(sink, *[qkv] * 7, gain, _band_penalty(hw))


def _attn_b_kernel(q_ref, kp_ref, kc_ref, kn_ref, vp_ref, vc_ref, vn_ref, pen_ref,
                   o_ref, lse_ref, *, hw):
    n_res, m = q_ref.shape[0], q_ref.shape[1]
    n_sub = m // Q_ROWS
    n_pairs = MIX_WIDTH // LANES
    rows_per_head = LANES // N_HEADS

    def group(r, sb, p):
        cols = slice(p * LANES, (p + 1) * LANES)
        return lambda: (q_ref[r, sb * Q_ROWS:(sb + 1) * Q_ROWS, cols],
                        _key_window(kp_ref.at[r], kc_ref.at[r], kn_ref.at[r], sb, hw, cols),
                        _key_window(vp_ref.at[r], vc_ref.at[r], vn_ref.at[r], sb, hw, cols),
                        pen_ref[_band_case(sb, n_sub, 2)], None)

    lse_rows = []

    def emit(g, out_t, lse):
        r, sb, p = g // (n_sub * n_pairs), (g // n_pairs) % n_sub, g % n_pairs
        rows = slice(sb * Q_ROWS, (sb + 1) * Q_ROWS)
        o_ref[r, rows, p * LANES:(p + 1) * LANES] = out_t.T.astype(BF16)
        lse_rows.extend([lse[:, 0:Q_ROWS], lse[:, Q_ROWS:2 * Q_ROWS]])
        if len(lse_rows) == N_HEADS:
            lse_t = jnp.concatenate(
                [jnp.broadcast_to(v, (rows_per_head, Q_ROWS)) for v in lse_rows], axis=0)
            lse_ref[r, rows, :] = lse_t.T
            lse_rows.clear()

    _attend_groups([group(r, sb, p) for r in range(n_res) for sb in range(n_sub)
                    for p in range(n_pairs)], emit)


def _attention_b_branch(qkv, window):
    b, dilation, seq, _ = qkv.shape
    w = MIX_WIDTH
    hw = window // (2 * dilation)
    m = min(seq, B_STEP_ROWS)
    n_res = min(dilation, B_STEP_ROWS // m)
    per = m // hw
    last = seq // hw - 1
    cur = lambda width, c: pl.BlockSpec((None, n_res, m, width), lambda i, r, j: (i, r, j, c))
    prev = lambda c: pl.BlockSpec((None, n_res, hw, w),
                                  lambda i, r, j: (i, r, jnp.maximum(j * per - 1, 0), c))
    nxt = lambda c: pl.BlockSpec((None, n_res, hw, w),
                                 lambda i, r, j: (i, r, jnp.minimum((j + 1) * per, last), c))
    kern = functools.partial(_attn_b_kernel, hw=hw)
    return pl.pallas_call(
        kern,
        grid=(b, dilation // n_res, seq // m),
        in_specs=[cur(w, 0), prev(1), cur(w, 1), nxt(1), prev(2), cur(w, 2), nxt(2),
                  _resident((3, Q_ROWS + 2 * hw, Q_ROWS), lambda i, r, j: (0, 0, 0))],
        out_specs=[cur(w, 0), cur(LANES, 0)],
        out_shape=[jax.ShapeDtypeStruct((b, dilation, seq, w), BF16),
                   jax.ShapeDtypeStruct((b, dilation, seq, LANES), F32)],
        compiler_params=_params(3),
        name=f"attn_b_d{dilation}",
    )(*[qkv] * 7, _band_penalty(hw))


def _split_bf16(v):
    hi = v.astype(BF16)
    rest = v - hi.astype(F32)
    mid = rest.astype(BF16)
    return hi, mid, (rest - mid.astype(F32)).astype(BF16)


def _merge_ffn_kernel(x_ref, mod_ref, ya_ref, o1_ref, o2_ref, o3_ref, l1_ref, l2_ref, l3_ref,
                      gb_ref, w_ref, perm2_ref, perm3_ref, expand_ref, g2_ref, wi_ref, wo_ref,
                      fg_ref, o_ref, relay_l, tmp_l, act_ref, *, final_norm):
    x = x_ref[0]
    tm = x.shape[0]

    def natural_order(src_ref, dil, dst, tmp, n_lane_groups):
        level = DILATIONS.index(dil)
        bufs = (tmp, dst)
        for lv in range(level, 0, -1):
            coarse, fine = DILATIONS[lv - 1], DILATIONS[lv]
            ratio, n_coarse, n_fine = fine // coarse, tm // coarse, tm // fine
            out = bufs[lv % 2]
            for rc in range(coarse):
                for q in range(ratio):
                    r = rc + coarse * q
                    for g in range(n_lane_groups):
                        if lv == level:
                            rows = src_ref[r, :, g * LANES:(g + 1) * LANES]
                        else:
                            rows = bufs[(lv + 1) % 2][g, r * n_fine:(r + 1) * n_fine, :]
                        out[g, pl.ds(rc * n_coarse + q, n_fine, stride=ratio), :] = rows

    lses = []
    for i, (l_ref, dil) in enumerate(zip((l1_ref, l2_ref, l3_ref), DILATIONS)):
        if dil == 1:
            lses.append(l_ref[0])
        else:
            natural_order(l_ref, dil, relay_l.at[i], tmp_l, 1)
            lses.append(relay_l[i, 0])
    mx = jnp.maximum(jnp.maximum(lses[0], lses[1]), lses[2])
    es = [jnp.exp2(l - mx) for l in lses]
    den = es[0] + es[1] + es[2]

    def head_weight(e):
        terms = jnp.concatenate(_split_bf16(e / den), axis=1)
        return jnp.dot(terms, expand_ref[...], preferred_element_type=F32)

    o1 = o1_ref[0].astype(F32)
    o2 = jnp.dot(perm2_ref[...], o2_ref[...].reshape(tm, MIX_WIDTH), preferred_element_type=F32)
    o3 = jnp.dot(perm3_ref[...], o3_ref[...].reshape(tm, MIX_WIDTH), preferred_element_type=F32)
    ob = o1 + head_weight(es[1]) * (o2 - o1) + head_weight(es[2]) * (o3 - o1)
    inv = lax.rsqrt(jnp.mean(ob * ob, axis=1, keepdims=True) + EPS)
    yb = ((ob * inv) * gb_ref[...]).astype(BF16)
    mix = jnp.dot(ya_ref[0], w_ref[0:MIX_WIDTH, :], preferred_element_type=F32)
    mix = mix + jnp.dot(yb, w_ref[MIX_WIDTH:2 * MIX_WIDTH, :], preferred_element_type=F32)
    x = x + mod_ref[0, 0, 5:6, :] * mix
    out = _ffn_step(x, mod_ref, 6, g2_ref, wi_ref, wo_ref, act_ref)
    if final_norm:
        out = out * lax.rsqrt(jnp.mean(out * out, axis=-1, keepdims=True) + EPS) * fg_ref[...]
    o_ref[0] = out


def _residue_permutation(tm, dil):
    token = np.arange(tm)
    perm = np.zeros((tm, tm), np.float32)
    perm[token, (token % dil) * (tm // dil) + token // dil] = 1.0
    return jnp.asarray(perm, BF16)


def _head_expansion(n_terms):
    expand = np.zeros((n_terms, LANES, MIX_WIDTH), np.float32)
    for h in range(N_HEADS):
        expand[:, h * (LANES // N_HEADS), h * HEAD_DIM:(h + 1) * HEAD_DIM] = 1.0
    return jnp.asarray(expand.reshape(n_terms * LANES, MIX_WIDTH), BF16)


def _merge_ffn(x, mod, layer, ya, outs, lses, gain_b, w_out, gain_ffn, wi, wo, final_gain,
               final_norm, tm=512):
    b, s, d = x.shape
    tok = lambda w: pl.BlockSpec((1, tm, w), lambda i, j: (i, j, 0))
    res = lambda dil, w: pl.BlockSpec((None, dil, tm // dil, w), lambda i, j: (i, 0, j, 0))
    const = lambda shape: _resident(shape, lambda i, j: (0, 0))
    perms = [_residue_permutation(tm, dil) for dil in DILATIONS[1:]]
    expand = _head_expansion(3)
    return pl.pallas_call(
        functools.partial(_merge_ffn_kernel, final_norm=final_norm),
        grid=(b, s // tm),
        in_specs=[
            tok(d),
            pl.BlockSpec((1, 1, N_MOD, d), lambda i, j: (layer, i, 0, 0)),
            tok(MIX_WIDTH),
            *[res(dil, MIX_WIDTH) for dil in DILATIONS],
            *[res(dil, LANES) for dil in DILATIONS],
            pl.BlockSpec((None, 1, MIX_WIDTH), lambda i, j: (layer, 0, 0)),
            _resident((None, 2 * MIX_WIDTH, d), lambda i, j: (layer, 0, 0)),
            const((tm, tm)), const((tm, tm)), const(expand.shape),
            *_ffn_specs(layer, d),
            pl.BlockSpec((1, d), lambda i, j: (0, 0)),
        ],
        out_specs=tok(d),
        out_shape=jax.ShapeDtypeStruct(x.shape, F32),
        scratch_shapes=[pltpu.VMEM((len(DILATIONS), 1, tm, LANES), F32),
                        pltpu.VMEM((1, tm, LANES), F32),
                        pltpu.VMEM((tm, D_FF), BF16)],
        compiler_params=_params(2),
        name="merge_out_ffn",
    )(x, mod, ya, *outs, *lses, gain_b, w_out, *perms, expand, gain_ffn, wi, wo, final_gain)


def _pair_kv_heads(t, axis):
    n_kv = KV_A_WIDTH // HEAD_DIM
    shape = t.shape
    split = shape[:axis] + (n_kv, N_HEADS // n_kv, HEAD_DIM) + shape[axis + 1:]
    return jnp.swapaxes(t.reshape(split), axis, axis + 1).reshape(shape)


def kernel(x, c, positions, ada_w, ada_b, norm_ffn1, ffn1_wi, ffn1_wo, norm_mix, w_in, sink,
           onorm_a, onorm_b, w_out, norm_ffn2, ffn2_wi, ffn2_wo, final_norm):
    depth = ada_w.shape[0]
    w_in16 = w_in.astype(BF16)
    a_kv, b_q, b_k = MIX_WIDTH + KV_A_WIDTH, MIX_WIDTH + 2 * KV_A_WIDTH, 2 * MIX_WIDTH + 2 * KV_A_WIDTH
    w_in_b = jnp.concatenate([
        _rotary_lane_order(_pair_kv_heads(w_in16[:, :, :MIX_WIDTH], 2)),
        _rotary_lane_order(w_in16[:, :, MIX_WIDTH:a_kv]),
        w_in16[:, :, a_kv:b_q],
        _rotary_lane_order(w_in16[:, :, b_q:b_k + MIX_WIDTH]),
        w_in16[:, :, b_k + MIX_WIDTH:]], axis=2)
    w_out_b = jnp.concatenate([_pair_kv_heads(w_out[:, :MIX_WIDTH].astype(BF16), 1),
                               w_out[:, MIX_WIDTH:].astype(BF16)], axis=1)
    gain_a = _pair_kv_heads(onorm_a, 1).reshape(depth, 1, MIX_WIDTH)
    gain_b = onorm_b.reshape(depth, 1, MIX_WIDTH)
    wi1, wo1 = ffn1_wi.astype(BF16), ffn1_wo.astype(BF16)
    wi2, wo2 = ffn2_wi.astype(BF16), ffn2_wo.astype(BF16)
    g_ffn1 = norm_ffn1.reshape(depth, 1, D_MODEL)
    g_mix = norm_mix.reshape(depth, 1, D_MODEL)
    g_ffn2 = norm_ffn2.reshape(depth, 1, D_MODEL)
    g_final = final_norm.reshape(1, D_MODEL)

    mod = _modulation(c, ada_w, ada_b)
    cos, sin = _rope_tables(positions)
    for l in range(depth):
        x, qkv_a, qkv_b = _ffn_project(x, mod, l, g_ffn1, wi1, wo1, g_mix, cos, sin, w_in_b)
        ya = _attention_a(qkv_a, sink, gain_a, l)
        outs, lses = zip(*[_attention_b_branch(qkv, w)
                           for qkv, (w, _) in zip(qkv_b, B_BRANCHES)])
        x = _merge_ffn(x, mod, l, ya, outs, lses, gain_b, w_out_b, g_ffn2, wi2, wo2, g_final,
                       l == depth - 1)
    return x
```

```python
import functools

import numpy as np
import jax
import jax.numpy as jnp
from jax import lax
from jax.experimental import pallas as pl
from jax.experimental.pallas import tpu as pltpu

D_MODEL = 1024
HEAD_DIM = 64
N_HEADS = 8
MIX_WIDTH = N_HEADS * HEAD_DIM
KV_A_WIDTH = 2 * HEAD_DIM
A_HALF_WINDOW = 128
B_BRANCHES = ((128, 1), (512, 4), (2048, 16))
DILATIONS = tuple(d for _, d in B_BRANCHES)
ROPE_THETA = 500000.0
ROPE_DIM = 16
D_FF = 2816
N_MOD = 9
D_IN = 2304
EPS = 1e-6
NEG_INF = -1e30
LOG2_E = 1.4426950408889634

LANES = 128
Q_ROWS = 128
STAGE_SKEW = 2
B_STEP_ROWS = 2048
SUB_ROWS = 256
F_CHUNK = 256
VMEM_LIMIT = 56 * 1024 * 1024

F32 = jnp.float32
BF16 = jnp.bfloat16


def _params(n_axes):
    return pltpu.CompilerParams(
        dimension_semantics=("arbitrary",) * n_axes, vmem_limit_bytes=VMEM_LIMIT)


def _resident(block_shape, index_map):
    return pl.BlockSpec(block_shape, index_map, pipeline_mode=pl.Buffered(1))


def _norm_mod(x, gain, shift, scale):
    y = x * lax.rsqrt(jnp.mean(x * x, axis=-1, keepdims=True) + EPS)
    return (y * gain) * (1.0 + scale) + shift


def _mod_kernel(c_ref, w_ref, b_ref, o_ref):
    ca = jax.nn.silu(c_ref[...]).astype(BF16)
    o_ref[0] = jnp.dot(ca, w_ref[0].astype(BF16), preferred_element_type=F32) + b_ref[0]


def _modulation(c, ada_w, ada_b):
    depth, d, n = ada_w.shape
    b = c.shape[0]
    tn = n // 8
    out = pl.pallas_call(
        _mod_kernel,
        grid=(depth, n // tn),
        in_specs=[
            pl.BlockSpec((b, d), lambda l, j: (0, 0)),
            pl.BlockSpec((1, d, tn), lambda l, j: (l, 0, j)),
            pl.BlockSpec((1, 1, tn), lambda l, j: (l, 0, j)),
        ],
        out_specs=pl.BlockSpec((1, b, tn), lambda l, j: (l, 0, j)),
        out_shape=jax.ShapeDtypeStruct((depth, b, n), F32),
        compiler_params=_params(2),
        name="adaln_mod",
    )(c, ada_w, ada_b.reshape(depth, 1, n))
    return out.reshape(depth, b, N_MOD, d)


def _ffn_step(x, mod_ref, mod_row, g_ref, wi_ref, wo_ref, act_ref):
    shift = mod_ref[0, 0, mod_row:mod_row + 1, :]
    scale = mod_ref[0, 0, mod_row + 1:mod_row + 2, :]
    gate_res = mod_ref[0, 0, mod_row + 2:mod_row + 3, :]
    hb = _norm_mod(x, g_ref[...], shift, scale).astype(BF16)
    for j in range(D_FF // F_CHUNK):
        lo = j * F_CHUNK
        gate = jnp.dot(hb, wi_ref[:, lo:lo + F_CHUNK], preferred_element_type=F32)
        up = jnp.dot(hb, wi_ref[:, D_FF + lo:D_FF + lo + F_CHUNK], preferred_element_type=F32)
        act_ref[:, lo:lo + F_CHUNK] = (jax.nn.silu(gate) * up).astype(BF16)
    y = jnp.dot(act_ref[...], wo_ref[...], preferred_element_type=F32)
    return x + (0.5 * gate_res) * y


def _ffn_specs(layer, d):
    return [pl.BlockSpec((None, 1, d), lambda *_: (layer, 0, 0)),
            _resident((None, d, 2 * D_FF), lambda *_: (layer, 0, 0)),
            _resident((None, D_FF, d), lambda *_: (layer, 0, 0))]


_ROT_HALF = ROPE_DIM // 2


def _rotary_lane_order(t):
    g = t.reshape(t.shape[:-1] + (t.shape[-1] // LANES, LANES))
    a, b = _ROT_HALF, HEAD_DIM
    g = jnp.concatenate([g[..., :a], g[..., b:b + a], g[..., 2 * a:b], g[..., a:2 * a],
                         g[..., b + a:]], axis=-1)
    return g.reshape(t.shape)


def _first_head_lanes(shape):
    lane = lax.broadcasted_iota(jnp.int32, shape, len(shape) - 1)
    return (lane < _ROT_HALF) | ((lane >= ROPE_DIM) & (lane < HEAD_DIM + _ROT_HALF))


def _rope_table_kernel(pos_ref, inv_ref, cos_ref, sin_ref):
    ts = pos_ref.shape[-1]
    ang = inv_ref[...] * pos_ref[0].astype(F32)
    cos_d, sin_d = jnp.cos(ang), jnp.sin(ang)
    plain = HEAD_DIM - ROPE_DIM
    ones, zeros = jnp.ones((plain, ts), F32), jnp.zeros((plain, ts), F32)
    cos_t = jnp.concatenate([cos_d, cos_d, ones, cos_d, cos_d, ones], axis=0)
    sin_t = jnp.concatenate([-sin_d, -sin_d, zeros, sin_d, sin_d, zeros], axis=0)
    cos_ref[0] = cos_t.T
    sin_ref[0] = sin_t.T


def _rope_tables(positions, ts=4096):
    b, s = positions.shape
    inv = (ROPE_THETA ** (-np.arange(0, ROPE_DIM, 2, dtype=np.float64) / ROPE_DIM)).astype(np.float32)
    out = pl.BlockSpec((1, ts, LANES), lambda i, j: (i, j, 0))
    return pl.pallas_call(
        _rope_table_kernel,
        grid=(b, s // ts),
        in_specs=[pl.BlockSpec((1, 1, ts), lambda i, j: (i, 0, j)),
                  pl.BlockSpec((_ROT_HALF, 1), lambda i, j: (0, 0))],
        out_specs=[out, out],
        out_shape=[jax.ShapeDtypeStruct((b, s, LANES), F32)] * 2,
        compiler_params=_params(2),
        name="rope_tables",
    )(positions.reshape(b, 1, s), jnp.asarray(inv.reshape(_ROT_HALF, 1)))


def _project_step(x, mod_ref, g_ref, cos_ref, sin_ref, w_ref, a_ref, b_refs, relays):
    n_dil = len(DILATIONS)
    tm = x.shape[0]
    hb = _norm_mod(x, g_ref[...], mod_ref[0, 0, 3:4, :], mod_ref[0, 0, 4:5, :]).astype(BF16)

    def rope(t):
        return t * cos_ref[0] + pltpu.roll(t, HEAD_DIM, axis=1) * sin_ref[0]

    proj = jnp.dot(hb, w_ref[...], preferred_element_type=F32)

    def lane_groups(col, width, rotary, mult):
        for g in range(width // LANES):
            tg = proj[:, col + g * LANES:col + (g + 1) * LANES]
            if rotary:
                tg = rope(tg)
            if mult is not None:
                tg = tg * mult
            yield g, tg

    scale = HEAD_DIM ** -0.5 * LOG2_E
    col = 0
    for width, rotary, mult in ((MIX_WIDTH, True, scale), (KV_A_WIDTH, True, None),
                                (KV_A_WIDTH, False, None)):
        for g, tg in lane_groups(col, width, rotary, mult):
            a_ref[0, :, col + g * LANES:col + (g + 1) * LANES] = tg.astype(BF16)
        col += width
    n_groups = MIX_WIDTH // LANES
    for t, (rotary, mult) in enumerate(((True, scale), (True, None), (False, None))):
        src, dst = relays[2 * t], relays[2 * t + 1]
        out_col = t * MIX_WIDTH
        for g, tg in lane_groups(col, MIX_WIDTH, rotary, mult):
            src[g] = tg
            b_refs[0][0, 0, :, out_col + g * LANES:out_col + (g + 1) * LANES] = tg.astype(BF16)
        for level in range(1, n_dil):
            coarse, fine = DILATIONS[level - 1], DILATIONS[level]
            ratio, n_coarse, n_fine = fine // coarse, tm // coarse, tm // fine
            for rc in range(coarse):
                for q in range(ratio):
                    for g in range(n_groups):
                        rows = src[g, pl.ds(rc * n_coarse + q, n_fine, stride=ratio), :]
                        r = rc + coarse * q
                        if level + 1 < n_dil:
                            dst[g, r * n_fine:(r + 1) * n_fine, :] = rows
                        b_refs[level][0, r, :, out_col + g * LANES:out_col + (g + 1) * LANES] = (
                            rows.astype(BF16))
            src, dst = dst, src
        col += MIX_WIDTH


def _ffn_proj_kernel(x_ref, mod_ref, g1_ref, wi_ref, wo_ref, gm_ref, cos_ref, sin_ref, w_ref,
                     xo_ref, *refs):
    n_dil = len(DILATIONS)
    x = _ffn_step(x_ref[0], mod_ref, 0, g1_ref, wi_ref, wo_ref, refs[1 + n_dil])
    xo_ref[0] = x
    _project_step(x, mod_ref, gm_ref, cos_ref, sin_ref, w_ref, refs[0], refs[1:1 + n_dil],
                  refs[2 + n_dil:])


def _ffn_project(x, mod, layer, gain_ffn, wi, wo, gain_mix, cos, sin, w_in, tm=512):
    b, s, d = x.shape
    tok = lambda w: pl.BlockSpec((1, tm, w), lambda i, j: (i, j, 0))
    res = lambda dil: pl.BlockSpec((1, dil, tm // dil, 3 * MIX_WIDTH),
                                   lambda i, j: (i, 0, j, 0))
    a_width = MIX_WIDTH + 2 * KV_A_WIDTH
    out_specs = [tok(d), tok(a_width)] + [res(dil) for dil in DILATIONS]
    out_shape = ([jax.ShapeDtypeStruct(x.shape, F32),
                  jax.ShapeDtypeStruct((b, s, a_width), BF16)]
                 + [jax.ShapeDtypeStruct((b, dil, s // dil, 3 * MIX_WIDTH), BF16)
                    for dil in DILATIONS])
    outs = pl.pallas_call(
        _ffn_proj_kernel,
        grid=(b, s // tm),
        in_specs=[
            tok(d),
            pl.BlockSpec((1, 1, N_MOD, d), lambda i, j: (layer, i, 0, 0)),
            *_ffn_specs(layer, d),
            pl.BlockSpec((None, 1, d), lambda i, j: (layer, 0, 0)),
            tok(LANES), tok(LANES),
            _resident((None, d, D_IN), lambda i, j: (layer, 0, 0)),
        ],
        out_specs=out_specs,
        out_shape=out_shape,
        scratch_shapes=[pltpu.VMEM((tm, D_FF), BF16)]
        + [pltpu.VMEM((MIX_WIDTH // LANES, tm, LANES), F32)] * 6,
        compiler_params=_params(2),
        name="ffn_in_proj",
    )(x, mod, gain_ffn, wi, wo, gain_mix, cos, sin, w_in)
    return outs[0], outs[1], outs[2:]


def _band_penalty(hw):
    n_keys = Q_ROWS + 2 * hw
    rel = np.arange(n_keys)[:, None] - np.arange(Q_ROWS)[None, :]
    band = (rel >= 0) & (rel <= 2 * hw)
    key = np.arange(n_keys)[:, None]
    valid = np.stack([band & (key >= hw), band, band & (key < n_keys - hw)])
    return jnp.asarray(np.where(valid, 0.0, NEG_INF), BF16)


def _key_window(prev_ref, cur_ref, next_ref, sb, hw, cols):
    m = cur_ref.shape[0]
    lo, hi = sb * Q_ROWS - hw, (sb + 1) * Q_ROWS + hw
    parts = [cur_ref[max(lo, 0):min(hi, m), cols]]
    if lo < 0:
        parts.insert(0, prev_ref[:, cols])
    if hi > m:
        parts.append(next_ref[:, cols])
    return parts[0] if len(parts) == 1 else jnp.concatenate(parts, axis=0)


def _band_case(sb, n_sub, step_axis):
    if sb == 0:
        return jnp.where(pl.program_id(step_axis) == 0, 0, 1)
    if sb == n_sub - 1:
        return jnp.where(pl.program_id(step_axis) == pl.num_programs(step_axis) - 1, 2, 1)
    return 1


def _attend_groups(groups, emit):
    n = len(groups)
    state = [None] * n
    row = lax.broadcasted_iota(jnp.int32, (2 * Q_ROWS, Q_ROWS), 0) & (Q_ROWS - 1)
    one_hot = (row == lax.broadcasted_iota(jnp.int32, (2 * Q_ROWS, Q_ROWS), 1)).astype(BF16)

    def scores(g):
        q_pair, k_win, v_win, penalty, sink_row = groups[g]()
        is_lo = _first_head_lanes(q_pair.shape)
        zero = jnp.zeros_like(q_pair)
        q_both = jnp.concatenate(
            [jnp.where(is_lo, q_pair, zero), jnp.where(is_lo, zero, q_pair)], axis=0)
        s = lax.dot_general(jnp.concatenate([k_win, penalty], axis=1),
                            jnp.concatenate([q_both, one_hot], axis=1),
                            (((1,), (1,)), ((), ())), preferred_element_type=F32)
        state[g] = (s, v_win, sink_row)

    def softmax(g):
        s, v_win, sink_row = state[g]
        m = jnp.max(s, axis=0, keepdims=True)
        if sink_row is not None:
            m = jnp.maximum(m, sink_row)
        e = jnp.exp2(s - m)
        l_keys = jnp.sum(e, axis=0, keepdims=True)
        denom = l_keys if sink_row is None else l_keys + jnp.exp2(sink_row - m)
        state[g] = (e.astype(BF16), v_win, denom, m + jnp.log2(l_keys))

    def values(g):
        e, v_win, denom, lse = state[g]
        pv = lax.dot_general(v_win, e, (((0,), (0,)), ((), ())), preferred_element_type=F32)
        inv = 1.0 / denom
        out_t = jnp.concatenate(
            [pv[0:HEAD_DIM, 0:Q_ROWS] * inv[:, 0:Q_ROWS],
             pv[HEAD_DIM:2 * HEAD_DIM, Q_ROWS:2 * Q_ROWS] * inv[:, Q_ROWS:]], axis=0)
        state[g] = None
        emit(g, out_t, lse)

    for t in range(n + 2 * STAGE_SKEW):
        if t < n:
            scores(t)
        if STAGE_SKEW <= t < n + STAGE_SKEW:
            softmax(t - STAGE_SKEW)
        if t >= 2 * STAGE_SKEW:
            values(t - 2 * STAGE_SKEW)


def _attn_a_kernel(sink_ref, q_ref, kp_ref, kc_ref, kn_ref, vp_ref, vc_ref, vn_ref, g_ref,
                   pen_ref, y_ref, *, layer):
    hw = A_HALF_WINDOW
    m = q_ref.shape[0]
    n_sub = m // Q_ROWS
    n_pairs = MIX_WIDTH // LANES
    is_lo_query = lax.broadcasted_iota(jnp.int32, (1, 2 * Q_ROWS), 1) < Q_ROWS
    all_cols = slice(0, KV_A_WIDTH)

    def group(sb, p):
        def load():
            sink_row = jnp.where(is_lo_query, sink_ref[layer, p],
                                 sink_ref[layer, p + n_pairs]) * LOG2_E
            return (q_ref[sb * Q_ROWS:(sb + 1) * Q_ROWS, p * LANES:(p + 1) * LANES],
                    _key_window(kp_ref, kc_ref, kn_ref, sb, hw, all_cols),
                    _key_window(vp_ref, vc_ref, vn_ref, sb, hw, all_cols),
                    pen_ref[_band_case(sb, n_sub, 1)], sink_row)
        return load

    pending = []

    def emit(g, out_t, _):
        pending.append(out_t)
        if len(pending) < n_pairs:
            return
        sb = g // n_pairs
        ssq = sum(jnp.sum(o * o, axis=0, keepdims=True) for o in pending)
        inv = lax.rsqrt(ssq / MIX_WIDTH + EPS)
        for p, o in enumerate(pending):
            y = (o * inv).T * g_ref[:, p * LANES:(p + 1) * LANES]
            y_ref[sb * Q_ROWS:(sb + 1) * Q_ROWS, p * LANES:(p + 1) * LANES] = y.astype(BF16)
        pending.clear()

    _attend_groups([group(sb, p) for sb in range(n_sub) for p in range(n_pairs)], emit)


def _attention_a(qkv, sink, gain, layer, m=2048):
    b, s, _ = qkv.shape
    hw = A_HALF_WINDOW
    per = m // hw
    last = s // hw - 1
    k_col, v_col = MIX_WIDTH // KV_A_WIDTH, MIX_WIDTH // KV_A_WIDTH + 1
    cur = lambda w, c: pl.BlockSpec((None, m, w), lambda i, j: (i, j, c))
    prev = lambda c: pl.BlockSpec((None, hw, KV_A_WIDTH),
                                  lambda i, j: (i, jnp.maximum(j * per - 1, 0), c))
    nxt = lambda c: pl.BlockSpec((None, hw, KV_A_WIDTH),
                                 lambda i, j: (i, jnp.minimum((j + 1) * per, last), c))
    kern = functools.partial(_attn_a_kernel, layer=layer)
    return pl.pallas_call(
        kern,
        grid=(b, s // m),
        in_specs=[
            pl.BlockSpec(memory_space=pltpu.SMEM),
            cur(MIX_WIDTH, 0),
            prev(k_col), cur(KV_A_WIDTH, k_col), nxt(k_col),
            prev(v_col), cur(KV_A_WIDTH, v_col), nxt(v_col),
            pl.BlockSpec((None, 1, MIX_WIDTH), lambda i, j: (layer, 0, 0)),
            _resident((3, Q_ROWS + 2 * hw, Q_ROWS), lambda i, j: (0, 0, 0)),
        ],
        out_specs=cur(MIX_WIDTH, 0),
        out_shape=jax.ShapeDtypeStruct((b, s, MIX_WIDTH), BF16),
        compiler_params=_params(2),
        name="attn_a",
    )(sink, *[qkv] * 7, gain, _band_penalty(hw))


def _attn_b_kernel(q_ref, kp_ref, kc_ref, kn_ref, vp_ref, vc_ref, vn_ref, pen_ref,
                   o_ref, lse_ref, *, hw):
    n_res, m = q_ref.shape[0], q_ref.shape[1]
    n_sub = m // Q_ROWS
    n_pairs = MIX_WIDTH // LANES
    rows_per_head = LANES // N_HEADS

    def group(r, sb, p):
        cols = slice(p * LANES, (p + 1) * LANES)
        return lambda: (q_ref[r, sb * Q_ROWS:(sb + 1) * Q_ROWS, cols],
                        _key_window(kp_ref.at[r], kc_ref.at[r], kn_ref.at[r], sb, hw, cols),
                        _key_window(vp_ref.at[r], vc_ref.at[r], vn_ref.at[r], sb, hw, cols),
                        pen_ref[_band_case(sb, n_sub, 2)], None)

    lse_rows = []

    def emit(g, out_t, lse):
        r, sb, p = g // (n_sub * n_pairs), (g // n_pairs) % n_sub, g % n_pairs
        rows = slice(sb * Q_ROWS, (sb + 1) * Q_ROWS)
        o_ref[r, rows, p * LANES:(p + 1) * LANES] = out_t.T.astype(BF16)
        lse_rows.extend([lse[:, 0:Q_ROWS], lse[:, Q_ROWS:2 * Q_ROWS]])
        if len(lse_rows) == N_HEADS:
            lse_t = jnp.concatenate(
                [jnp.broadcast_to(v, (rows_per_head, Q_ROWS)) for v in lse_rows], axis=0)
            lse_ref[r, rows, :] = lse_t.T
            lse_rows.clear()

    _attend_groups([group(r, sb, p) for r in range(n_res) for sb in range(n_sub)
                    for p in range(n_pairs)], emit)


def _attention_b_branch(qkv, window):
    b, dilation, seq, _ = qkv.shape
    w = MIX_WIDTH
    hw = window // (2 * dilation)
    m = min(seq, B_STEP_ROWS)
    n_res = min(dilation, B_STEP_ROWS // m)
    per = m // hw
    last = seq // hw - 1
    cur = lambda width, c: pl.BlockSpec((None, n_res, m, width), lambda i, r, j: (i, r, j, c))
    prev = lambda c: pl.BlockSpec((None, n_res, hw, w),
                                  lambda i, r, j: (i, r, jnp.maximum(j * per - 1, 0), c))
    nxt = lambda c: pl.BlockSpec((None, n_res, hw, w),
                                 lambda i, r, j: (i, r, jnp.minimum((j + 1) * per, last), c))
    kern = functools.partial(_attn_b_kernel, hw=hw)
    return pl.pallas_call(
        kern,
        grid=(b, dilation // n_res, seq // m),
        in_specs=[cur(w, 0), prev(1), cur(w, 1), nxt(1), prev(2), cur(w, 2), nxt(2),
                  _resident((3, Q_ROWS + 2 * hw, Q_ROWS), lambda i, r, j: (0, 0, 0))],
        out_specs=[cur(w, 0), cur(LANES, 0)],
        out_shape=[jax.ShapeDtypeStruct((b, dilation, seq, w), BF16),
                   jax.ShapeDtypeStruct((b, dilation, seq, LANES), F32)],
        compiler_params=_params(3),
        name=f"attn_b_d{dilation}",
    )(*[qkv] * 7, _band_penalty(hw))


def _split_bf16(v):
    hi = v.astype(BF16)
    rest = v - hi.astype(F32)
    mid = rest.astype(BF16)
    return hi, mid, (rest - mid.astype(F32)).astype(BF16)


def _merge_ffn_kernel(x_ref, mod_ref, ya_ref, o1_ref, o2_ref, o3_ref, l1_ref, l2_ref, l3_ref,
                      gb_ref, w_ref, perm2_ref, perm3_ref, expand_ref, g2_ref, wi_ref, wo_ref,
                      fg_ref, o_ref, relay_l, tmp_l, act_ref, *, final_norm):
    sub = perm2_ref.shape[0]

    def natural_order(src_ref, dil, dst, tmp, n_lane_groups, row0):
        level = DILATIONS.index(dil)
        bufs = (tmp, dst)
        for lv in range(level, 0, -1):
            coarse, fine = DILATIONS[lv - 1], DILATIONS[lv]
            ratio, n_coarse, n_fine = fine // coarse, sub // coarse, sub // fine
            out = bufs[lv % 2]
            for rc in range(coarse):
                for q in range(ratio):
                    r = rc + coarse * q
                    for g in range(n_lane_groups):
                        if lv == level:
                            rows = src_ref[r, row0 // dil:(row0 + sub) // dil,
                                           g * LANES:(g + 1) * LANES]
                        else:
                            rows = bufs[(lv + 1) % 2][g, r * n_fine:(r + 1) * n_fine, :]
                        out[g, pl.ds(rc * n_coarse + q, n_fine, stride=ratio), :] = rows

    def head_weight(w):
        terms = jnp.concatenate(_split_bf16(w), axis=1)
        return jnp.dot(terms, expand_ref[...], preferred_element_type=F32)

    def residue_rows(o_ref_d, dil, row0):
        return o_ref_d[:, row0 // dil:(row0 + sub) // dil, :].reshape(sub, MIX_WIDTH)

    for h in range(x_ref.shape[1] // sub):
        row0 = h * sub
        rows = slice(row0, row0 + sub)
        lses = []
        for i, (l_ref, dil) in enumerate(zip((l1_ref, l2_ref, l3_ref), DILATIONS)):
            if dil == 1:
                lses.append(l_ref[0, rows, :])
            else:
                natural_order(l_ref, dil, relay_l.at[h, i], tmp_l.at[h], 1, row0)
                lses.append(relay_l[h, i, 0])
        mx = jnp.maximum(jnp.maximum(lses[0], lses[1]), lses[2])
        es = [jnp.exp2(l - mx) for l in lses]
        den = es[0] + es[1] + es[2]
        o1 = o1_ref[0, rows, :].astype(F32)
        o2 = jnp.dot(perm2_ref[...], residue_rows(o2_ref, DILATIONS[1], row0),
                     preferred_element_type=F32)
        o3 = jnp.dot(perm3_ref[...], residue_rows(o3_ref, DILATIONS[2], row0),
                     preferred_element_type=F32)
        ob = o1 + head_weight(es[1] / den) * (o2 - o1) + head_weight(es[2] / den) * (o3 - o1)
        inv = lax.rsqrt(jnp.mean(ob * ob, axis=1, keepdims=True) + EPS)
        yb = ((ob * inv) * gb_ref[...]).astype(BF16)
        mix = jnp.dot(ya_ref[0, rows, :], w_ref[0:MIX_WIDTH, :], preferred_element_type=F32)
        mix = mix + jnp.dot(yb, w_ref[MIX_WIDTH:2 * MIX_WIDTH, :], preferred_element_type=F32)
        x = x_ref[0, rows, :] + mod_ref[0, 0, 5:6, :] * mix
        out = _ffn_step(x, mod_ref, 6, g2_ref, wi_ref, wo_ref, act_ref.at[rows])
        if final_norm:
            out = out * lax.rsqrt(jnp.mean(out * out, axis=-1, keepdims=True) + EPS) * fg_ref[...]
        o_ref[0, rows, :] = out


def _residue_permutation(tm, dil):
    token = np.arange(tm)
    perm = np.zeros((tm, tm), np.float32)
    perm[token, (token % dil) * (tm // dil) + token // dil] = 1.0
    return jnp.asarray(perm, BF16)


def _head_expansion(n_terms):
    expand = np.zeros((n_terms, LANES, MIX_WIDTH), np.float32)
    for h in range(N_HEADS):
        expand[:, h * (LANES // N_HEADS), h * HEAD_DIM:(h + 1) * HEAD_DIM] = 1.0
    return jnp.asarray(expand.reshape(n_terms * LANES, MIX_WIDTH), BF16)


def _merge_ffn(x, mod, layer, ya, outs, lses, gain_b, w_out, gain_ffn, wi, wo, final_gain,
               final_norm, tm=512):
    b, s, d = x.shape
    sub, n_sub = SUB_ROWS, tm // SUB_ROWS
    tok = lambda w: pl.BlockSpec((1, tm, w), lambda i, j: (i, j, 0))
    res = lambda dil, w: pl.BlockSpec((None, dil, tm // dil, w), lambda i, j: (i, 0, j, 0))
    const = lambda shape: _resident(shape, lambda i, j: (0, 0))
    perms = [_residue_permutation(sub, dil) for dil in DILATIONS[1:]]
    expand = _head_expansion(3)
    return pl.pallas_call(
        functools.partial(_merge_ffn_kernel, final_norm=final_norm),
        grid=(b, s // tm),
        in_specs=[
            tok(d),
            pl.BlockSpec((1, 1, N_MOD, d), lambda i, j: (layer, i, 0, 0)),
            tok(MIX_WIDTH),
            *[res(dil, MIX_WIDTH) for dil in DILATIONS],
            *[res(dil, LANES) for dil in DILATIONS],
            pl.BlockSpec((None, 1, MIX_WIDTH), lambda i, j: (layer, 0, 0)),
            _resident((None, 2 * MIX_WIDTH, d), lambda i, j: (layer, 0, 0)),
            const((sub, sub)), const((sub, sub)), const(expand.shape),
            *_ffn_specs(layer, d),
            pl.BlockSpec((1, d), lambda i, j: (0, 0)),
        ],
        out_specs=tok(d),
        out_shape=jax.ShapeDtypeStruct(x.shape, F32),
        scratch_shapes=[pltpu.VMEM((n_sub, len(DILATIONS), 1, sub, LANES), F32),
                        pltpu.VMEM((n_sub, 1, sub, LANES), F32),
                        pltpu.VMEM((tm, D_FF), BF16)],
        compiler_params=_params(2),
        name="merge_out_ffn",
    )(x, mod, ya, *outs, *lses, gain_b, w_out, *perms, expand, gain_ffn, wi, wo, final_gain)


def _pair_kv_heads(t, axis):
    n_kv = KV_A_WIDTH // HEAD_DIM
    shape = t.shape
    split = shape[:axis] + (n_kv, N_HEADS // n_kv, HEAD_DIM) + shape[axis + 1:]
    return jnp.swapaxes(t.reshape(split), axis, axis + 1).reshape(shape)


def kernel(x, c, positions, ada_w, ada_b, norm_ffn1, ffn1_wi, ffn1_wo, norm_mix, w_in, sink,
           onorm_a, onorm_b, w_out, norm_ffn2, ffn2_wi, ffn2_wo, final_norm):
    depth = ada_w.shape[0]
    w_in16 = w_in.astype(BF16)
    a_kv, b_q, b_k = MIX_WIDTH + KV_A_WIDTH, MIX_WIDTH + 2 * KV_A_WIDTH, 2 * MIX_WIDTH + 2 * KV_A_WIDTH
    w_in_b = jnp.concatenate([
        _rotary_lane_order(_pair_kv_heads(w_in16[:, :, :MIX_WIDTH], 2)),
        _rotary_lane_order(w_in16[:, :, MIX_WIDTH:a_kv]),
        w_in16[:, :, a_kv:b_q],
        _rotary_lane_order(w_in16[:, :, b_q:b_k + MIX_WIDTH]),
        w_in16[:, :, b_k + MIX_WIDTH:]], axis=2)
    w_out_b = jnp.concatenate([_pair_kv_heads(w_out[:, :MIX_WIDTH].astype(BF16), 1),
                               w_out[:, MIX_WIDTH:].astype(BF16)], axis=1)
    gain_a = _pair_kv_heads(onorm_a, 1).reshape(depth, 1, MIX_WIDTH)
    gain_b = onorm_b.reshape(depth, 1, MIX_WIDTH)
    wi1, wo1 = ffn1_wi.astype(BF16), ffn1_wo.astype(BF16)
    wi2, wo2 = ffn2_wi.astype(BF16), ffn2_wo.astype(BF16)
    g_ffn1 = norm_ffn1.reshape(depth, 1, D_MODEL)
    g_mix = norm_mix.reshape(depth, 1, D_MODEL)
    g_ffn2 = norm_ffn2.reshape(depth, 1, D_MODEL)
    g_final = final_norm.reshape(1, D_MODEL)

    mod = _modulation(c, ada_w, ada_b)
    cos, sin = _rope_tables(positions)
    for l in range(depth):
        x, qkv_a, qkv_b = _ffn_project(x, mod, l, g_ffn1, wi1, wo1, g_mix, cos, sin, w_in_b)
        ya = _attention_a(qkv_a, sink, gain_a, l)
        outs, lses = zip(*[_attention_b_branch(qkv, w)
                           for qkv, (w, _) in zip(qkv_b, B_BRANCHES)])
        x = _merge_ffn(x, mod, l, ya, outs, lses, gain_b, w_out_b, g_ffn2, wi2, wo2, g_final,
                       l == depth - 1)
    return x
```
